```python
import math
import jax, jax.numpy as jnp
from jax import lax
import numpy as np

D_MODEL = 1024
BATCH = 8
SEQ = 4096
DEPTH = 4

N_META = 16
BLOCK = 128
PAD = (-N_META) % BLOCK
MLA_HEADS = 8
MLA_Q_RANK = 256
MLA_KV_RANK = 128
MLA_NOPE = 64
MLA_ROPE = 32
MLA_V = 64
ROPE_THETA = 10000.0
SWA_HEADS = 8
SWA_KV_HEADS = 2
SWA_GROUP = SWA_HEADS // SWA_KV_HEADS
SWA_HEAD_DIM = 64
WINDOW = 128
MIX_A = MLA_HEADS * MLA_V
MIX_B = SWA_HEADS * SWA_HEAD_DIM
MIX_WIDTH = MIX_A + MIX_B
IN_SIZES = [MLA_Q_RANK, MLA_KV_RANK, MLA_ROPE,
            SWA_HEADS * SWA_HEAD_DIM, SWA_KV_HEADS * SWA_HEAD_DIM, SWA_KV_HEADS * SWA_HEAD_DIM]
IN_COLS = sum(IN_SIZES)
IN_SPLITS = [int(v) for v in np.cumsum(IN_SIZES)[:-1]]
D_FF = -(-8 * D_MODEL // (3 * 256)) * 256
ALPHA = (2 * DEPTH) ** 0.25
BETA = (8 * DEPTH) ** -0.25
LN_EPS = 1e-5
RMS_EPS = 1e-6
NEG = -1e30

kernel_name = "hymba_mla_swa_deepnorm_trunk"


def layer_norm(x, g, b):
    xf = x.astype(jnp.float32)
    mu = jnp.mean(xf, -1, keepdims=True)
    var = jnp.mean(jnp.square(xf - mu), -1, keepdims=True)
    return ((xf - mu) * lax.rsqrt(var + LN_EPS) * g + b).astype(x.dtype)


def rms_norm(x, g):
    xf = x.astype(jnp.float32)
    return (xf * lax.rsqrt(jnp.mean(jnp.square(xf), -1, keepdims=True) + RMS_EPS) * g).astype(x.dtype)


def rope_tables(T):
    pos = jnp.arange(T, dtype=jnp.float32)
    inv = ROPE_THETA ** (-jnp.arange(0, MLA_ROPE, 2, dtype=jnp.float32) / MLA_ROPE)
    ang = pos[:, None] * inv[None, :]
    return jnp.cos(ang), jnp.sin(ang)


def apply_rope(x, cos, sin):
    x1, x2 = jnp.split(x.astype(jnp.float32), 2, axis=-1)
    return jnp.concatenate([x1 * cos - x2 * sin, x2 * cos + x1 * sin], -1).astype(x.dtype)


def alibi_slopes():
    h = jnp.arange(1, SWA_HEADS + 1, dtype=jnp.float32)
    return jnp.exp2(-8.0 * h / SWA_HEADS)


def pad_front(a):
    return jnp.pad(a, [(0, 0), (PAD, 0)] + [(0, 0)] * (a.ndim - 2))


def mla_mixer(q_c, kv_c, k_r, q_norm_g, w_uq, kv_norm_g, w_ukv, cos, sin):
    B, T, _ = q_c.shape
    q = (rms_norm(q_c, q_norm_g) @ w_uq).reshape(B, T, MLA_HEADS, MLA_NOPE + MLA_ROPE)
    q_nope, q_rope = q[..., :MLA_NOPE], q[..., MLA_NOPE:]
    q_rope = apply_rope(q_rope, cos[:, None, :], sin[:, None, :])
    kv = (rms_norm(kv_c, kv_norm_g) @ w_ukv).reshape(B, T, MLA_HEADS, MLA_NOPE + MLA_V)
    k_nope, v = kv[..., :MLA_NOPE], kv[..., MLA_NOPE:]
    k_rope = apply_rope(k_r, cos, sin)
    q_nope, q_rope, k_nope, k_rope, v = (pad_front(a) for a in (q_nope, q_rope, k_nope, k_rope, v))
    Lp = T + PAD
    nb = Lp // BLOCK
    key_pos = jnp.arange(Lp)
    key_valid = key_pos >= PAD
    scale = (MLA_NOPE + MLA_ROPE) ** -0.5
    qn_b = q_nope.reshape(B, nb, BLOCK, MLA_HEADS, MLA_NOPE).transpose(1, 0, 2, 3, 4)
    qr_b = q_rope.reshape(B, nb, BLOCK, MLA_HEADS, MLA_ROPE).transpose(1, 0, 2, 3, 4)

    def query_block(args):
        i, qn, qr = args
        s = (jnp.einsum('bqhd,bkhd->bhqk', qn, k_nope)
             + jnp.einsum('bqhr,bkr->bhqk', qr, k_rope)).astype(jnp.float32) * scale
        q_pos = i * BLOCK + jnp.arange(BLOCK)
        mask = (key_pos[None, :] <= q_pos[:, None]) & key_valid[None, :]
        p = jax.nn.softmax(jnp.where(mask, s, NEG), axis=-1)
        return jnp.einsum('bhqk,bkhd->bqhd', p.astype(v.dtype), v)

    out = lax.map(query_block, (jnp.arange(nb), qn_b, qr_b))
    out = out.transpose(1, 0, 2, 3, 4).reshape(B, Lp, MIX_A)
    return out[:, PAD:]


def swa_mixer(q, k, v, sinks, slopes):
    B, T, _ = q.shape
    q = pad_front(q.reshape(B, T, SWA_KV_HEADS, SWA_GROUP, SWA_HEAD_DIM))
    k = pad_front(k.reshape(B, T, SWA_KV_HEADS, SWA_HEAD_DIM))
    v = pad_front(v.reshape(B, T, SWA_KV_HEADS, SWA_HEAD_DIM))
    Lp = T + PAD
    nb = Lp // BLOCK
    qb = q.reshape(B, nb, BLOCK, SWA_KV_HEADS, SWA_GROUP, SWA_HEAD_DIM)
    kb = k.reshape(B, nb, BLOCK, SWA_KV_HEADS, SWA_HEAD_DIM)
    vb = v.reshape(B, nb, BLOCK, SWA_KV_HEADS, SWA_HEAD_DIM)
    shift = [(0, 0), (1, 0), (0, 0), (0, 0), (0, 0)]
    k_band = jnp.concatenate([jnp.pad(kb, shift)[:, :-1], kb], axis=2)
    v_band = jnp.concatenate([jnp.pad(vb, shift)[:, :-1], vb], axis=2)
    s = jnp.einsum('bnqkgd,bnskd->bnkgqs', qb, k_band).astype(jnp.float32) * (SWA_HEAD_DIM ** -0.5)
    dist = (BLOCK + jnp.arange(BLOCK))[:, None] - jnp.arange(2 * BLOCK)[None, :]
    band = (dist >= 0) & (dist < WINDOW)
    key_idx = jnp.arange(nb)[:, None] * BLOCK - BLOCK + jnp.arange(2 * BLOCK)[None, :]
    mask = band[None] & (key_idx >= PAD)[:, None, :]
    alibi = -slopes.reshape(SWA_KV_HEADS, SWA_GROUP)[:, :, None, None] * dist.astype(jnp.float32)
    s = jnp.where(mask[None, :, None, None], s + alibi, NEG)
    sink = sinks.astype(jnp.float32).reshape(SWA_KV_HEADS, SWA_GROUP)[None, None, :, :, None, None]
    m = jnp.maximum(jnp.max(s, -1, keepdims=True), sink)
    e = jnp.exp(s - m)
    p = e / (jnp.sum(e, -1, keepdims=True) + jnp.exp(sink - m))
    out = jnp.einsum('bnkgqs,bnskd->bnqkgd', p.astype(v_band.dtype), v_band)
    return out.reshape(B, Lp, MIX_B)[:, PAD:]


def setup_inputs(seed: int = 0) -> dict:
    key = jax.random.key(seed)
    ks = jax.random.split(key, 20)
    nrm = lambda k, shape, scale: jax.random.normal(k, shape, jnp.float32) * scale
    gain = lambda k, shape: 1.0 + nrm(k, shape, 0.02)
    return {
        "x": nrm(ks[0], (BATCH, SEQ, D_MODEL), 1.0),
        "meta_tokens": nrm(ks[1], (N_META, D_MODEL), 1.0),
        "ln_in_g": gain(ks[2], (D_MODEL,)),
        "ln_in_b": nrm(ks[3], (D_MODEL,), 0.02),
        "w_in": nrm(ks[4], (DEPTH, D_MODEL, IN_COLS), D_MODEL ** -0.5),
        "q_norm_g": gain(ks[5], (DEPTH, MLA_Q_RANK)),
        "w_uq": nrm(ks[6], (DEPTH, MLA_Q_RANK, MLA_HEADS * (MLA_NOPE + MLA_ROPE)), MLA_Q_RANK ** -0.5),
        "kv_norm_g": gain(ks[7], (DEPTH, MLA_KV_RANK)),
        "w_ukv": nrm(ks[8], (DEPTH, MLA_KV_RANK, MLA_HEADS * (MLA_NOPE + MLA_V)), MLA_KV_RANK ** -0.5),
        "attn_sinks": nrm(ks[9], (DEPTH, SWA_HEADS), 1.0),
        "grp_norm_a": gain(ks[10], (DEPTH, MIX_A)),
        "grp_norm_b": gain(ks[11], (DEPTH, MIX_B)),
        "w_out": nrm(ks[12], (DEPTH, MIX_WIDTH, D_MODEL), BETA * MIX_WIDTH ** -0.5),
        "ln1_g": gain(ks[13], (DEPTH, D_MODEL)),
        "ln1_b": nrm(ks[14], (DEPTH, D_MODEL), 0.02),
        "w_gate_up": nrm(ks[15], (DEPTH, D_MODEL, 2 * D_FF), D_MODEL ** -0.5),
        "w_down": nrm(ks[16], (DEPTH, D_FF, D_MODEL), BETA * D_FF ** -0.5),
        "ln2_g": gain(ks[17], (DEPTH, D_MODEL)),
        "ln2_b": nrm(ks[18], (DEPTH, D_MODEL), 0.02),
    }


def reference(x, meta_tokens, ln_in_g, ln_in_b, w_in, q_norm_g, w_uq, kv_norm_g, w_ukv,
              attn_sinks, grp_norm_a, grp_norm_b, w_out, ln1_g, ln1_b, w_gate_up, w_down,
              ln2_g, ln2_b):
    B = x.shape[0]
    meta = jnp.broadcast_to(meta_tokens[None].astype(x.dtype), (B, N_META, D_MODEL))
    h = layer_norm(jnp.concatenate([meta, x], axis=1), ln_in_g, ln_in_b)
    T = h.shape[1]
    cos, sin = rope_tables(T)
    slopes = alibi_slopes()
    for l in range(DEPTH):
        proj = h @ w_in[l]
        q_c, kv_c, k_r, sq, sk, sv = jnp.split(proj, IN_SPLITS, axis=-1)
        o_a = mla_mixer(q_c, kv_c, k_r, q_norm_g[l], w_uq[l], kv_norm_g[l], w_ukv[l], cos, sin)
        o_b = swa_mixer(sq, sk, sv, attn_sinks[l], slopes)
        o = jnp.concatenate([rms_norm(o_a, grp_norm_a[l]), rms_norm(o_b, grp_norm_b[l])], axis=-1)
        h = layer_norm(ALPHA * h + o @ w_out[l], ln1_g[l], ln1_b[l])
        gate, up = jnp.split(h @ w_gate_up[l], 2, axis=-1)
        f = (jax.nn.silu(gate) * up) @ w_down[l]
        h = layer_norm(ALPHA * h + f, ln2_g[l], ln2_b[l])
    return h[:, N_META:]
```

```python
import functools

import jax
import jax.numpy as jnp
import numpy as np
from jax import lax
from jax.experimental import pallas as pl
from jax.experimental.pallas import tpu as pltpu

D_MODEL = 1024
DEPTH = 4
N_META = 16
BLOCK = 128
PAD = BLOCK - N_META
MLA_HEADS = 8
MLA_Q_RANK = 256
MLA_KV_RANK = 128
MLA_NOPE = 64
MLA_ROPE = 32
MLA_V = 64
ROPE_THETA = 10000.0
SWA_HEADS = 8
SWA_KV_HEADS = 2
SWA_GROUP = SWA_HEADS // SWA_KV_HEADS
SWA_HEAD_DIM = 64
WINDOW = 128
MIX_A = MLA_HEADS * MLA_V
MIX_B = SWA_HEADS * SWA_HEAD_DIM
D_FF = 2816
ALPHA = (2 * DEPTH) ** 0.25
LN_EPS = 1e-5
RMS_EPS = 1e-6
NEG = -1e30

LANES = 128
HEAD_PAD = LANES
ROPE_HALF = MLA_ROPE // 2
ATT_TILE = 256
IN_COLS_P = MLA_Q_RANK + LANES + LANES + MIX_B + LANES + LANES

BF16 = jnp.bfloat16
F32 = jnp.float32


def _row_tile(lp):
    for t in (512, 384, 256, 128):
        if lp % t == 0:
            return t
    raise ValueError(f"padded length {lp} is not a multiple of {BLOCK}")


def _layer_norm(z, g, b):
    mu = jnp.mean(z, -1, keepdims=True)
    zc = z - mu
    var = jnp.mean(zc * zc, -1, keepdims=True)
    return zc * lax.rsqrt(var + LN_EPS) * g + b


def _rms_norm(z, g):
    return z * lax.rsqrt(jnp.mean(z * z, -1, keepdims=True) + RMS_EPS) * g


def _dot(a, b):
    return jnp.dot(a, b, preferred_element_type=F32)


def _dot_nt(a, b):
    return lax.dot_general(a, b, (((1,), (1,)), ((), ())), preferred_element_type=F32)


def _ln_in_kernel(x_ref, meta_ref, g_ref, b_ref, o_ref):
    i = pl.program_id(1)
    z = jnp.where(i == 0, meta_ref[...], x_ref[...])
    o_ref[...] = _layer_norm(z, g_ref[...], b_ref[...])


def _ln_in(x, meta_pad, g, b):
    bsz, s, d = x.shape
    nb = s // BLOCK + 1
    return pl.pallas_call(
        _ln_in_kernel,
        grid=(bsz, nb),
        in_specs=[
            pl.BlockSpec((None, BLOCK, d), lambda bi, i: (bi, jnp.maximum(i - 1, 0), 0)),
            pl.BlockSpec((BLOCK, d), lambda bi, i: (0, 0)),
            pl.BlockSpec((1, d), lambda bi, i: (0, 0)),
            pl.BlockSpec((1, d), lambda bi, i: (0, 0)),
        ],
        out_specs=pl.BlockSpec((None, BLOCK, d), lambda bi, i: (bi, i, 0)),
        out_shape=jax.ShapeDtypeStruct((bsz, nb * BLOCK, d), F32),
        name="ln_in",
    )(x, meta_pad, g, b)


def _rope(z, c, sa, sb):
    return z * c + pltpu.roll(z, LANES - ROPE_HALF, 1) * sa + pltpu.roll(z, ROPE_HALF, 1) * sb


def _proj_kernel(h_ref, w_in_ref, gq_ref, w_uq_ref, gkv_ref, w_ukv_ref, c_ref, sa_ref, sb_ref,
                 q_ref, k_ref, v_ref, sq_ref, sk_ref, sv_ref):
    x = h_ref[...].astype(BF16)
    proj = _dot(x, w_in_ref[...])
    c, sa, sb = c_ref[...], sa_ref[...], sb_ref[...]
    o = 0
    q_c = proj[:, o:o + MLA_Q_RANK]; o += MLA_Q_RANK
    kv_c = proj[:, o:o + MLA_KV_RANK]; o += MLA_KV_RANK
    k_r = proj[:, o:o + LANES]; o += LANES
    s_q = proj[:, o:o + MIX_B]; o += MIX_B
    s_k = proj[:, o:o + LANES]; o += LANES
    s_v = proj[:, o:o + LANES]

    q = _dot(_rms_norm(q_c, gq_ref[...]).astype(BF16), w_uq_ref[...])
    q_scale = (MLA_NOPE + MLA_ROPE) ** -0.5
    for h in range(MLA_HEADS):
        sl = slice(h * HEAD_PAD, (h + 1) * HEAD_PAD)
        q_ref[:, sl] = (_rope(q[:, sl], c, sa, sb) * q_scale).astype(BF16)

    kv = _dot(_rms_norm(kv_c, gkv_ref[...]).astype(BF16), w_ukv_ref[...])
    k_rope = _rope(k_r, c, sa, sb)
    for h in range(MLA_HEADS):
        sl = slice(h * HEAD_PAD, (h + 1) * HEAD_PAD)
        k_ref[:, sl] = (kv[:, sl] + k_rope).astype(BF16)
    v_ref[...] = kv[:, MLA_HEADS * HEAD_PAD:].astype(BF16)

    sq_ref[...] = (s_q * (SWA_HEAD_DIM ** -0.5)).astype(BF16)
    low = lax.broadcasted_iota(jnp.int32, s_k.shape, 1) < SWA_HEAD_DIM
    for src, dst in ((s_k, sk_ref), (s_v, sv_ref)):
        swapped = pltpu.roll(src, SWA_HEAD_DIM, 1)
        dst[:, :LANES] = jnp.where(low, src, swapped).astype(BF16)
        dst[:, LANES:] = jnp.where(low, swapped, src).astype(BF16)


def _proj(h, w_in_p, gq, w_uq_p, gkv, w_ukv_p, c, sa, sb, tm):
    bsz, lp, d = h.shape
    nt = lp // tm
    row = lambda w: pl.BlockSpec((None, tm, w), lambda bi, i: (bi, i, 0))
    full = lambda a: pl.BlockSpec(a.shape, lambda bi, i: (0,) * a.ndim)
    tab = pl.BlockSpec((tm, LANES), lambda bi, i: (i, 0))
    widths = (MLA_HEADS * HEAD_PAD, MLA_HEADS * HEAD_PAD, MIX_A, MIX_B, 2 * LANES, 2 * LANES)
    return pl.pallas_call(
        _proj_kernel,
        grid=(bsz, nt),
        in_specs=[row(d), full(w_in_p), full(gq), full(w_uq_p), full(gkv), full(w_ukv_p), tab, tab, tab],
        out_specs=[row(w) for w in widths],
        out_shape=[jax.ShapeDtypeStruct((bsz, lp, w), BF16) for w in widths],
        name="proj",
    )(h, w_in_p, gq, w_uq_p, gkv, w_ukv_p, c, sa, sb)


def _mla_kernel(q_ref, k_ref, v_ref, o_ref, *, n_tiles):
    low = lax.broadcasted_iota(jnp.int32, (1, LANES), 1) < MLA_V

    def scores(qh, kh, mask):
        s = _dot_nt(qh, kh)
        return s if mask is None else jnp.where(mask, s, NEG)

    def first(qh, kh, vc, mask):
        s = scores(qh, kh, mask)
        m = jnp.max(s, -1, keepdims=True)
        p = jnp.exp(s - m)
        return m, jnp.sum(p, -1, keepdims=True), _dot(p.astype(BF16), vc)

    def update(state, qh, kh, vc, mask):
        m, l, acc = state
        s = scores(qh, kh, mask)
        m_new = jnp.maximum(m, jnp.max(s, -1, keepdims=True))
        a = jnp.exp(m - m_new)
        p = jnp.exp(s - m_new)
        return m_new, a * l + jnp.sum(p, -1, keepdims=True), a * acc + _dot(p.astype(BF16), vc)

    def finish(states):
        (_, l0, acc0), (_, l1, acc1) = states
        return jnp.where(low, acc0 / l0, acc1 / l1).astype(BF16)

    def heads(x):
        return x[:, :HEAD_PAD], x[:, HEAD_PAD:]

    r = lax.broadcasted_iota(jnp.int32, (BLOCK, BLOCK), 0)
    cidx = lax.broadcasted_iota(jnp.int32, (BLOCK, BLOCK), 1)
    mask0 = (cidx <= r) & (cidx >= PAD)
    q0, k0, v0 = q_ref[:BLOCK, :], k_ref[:BLOCK, :], v_ref[:BLOCK, :]
    o_ref[:BLOCK, :] = finish([first(qh, kh, v0, mask0) for qh, kh in zip(heads(q0), heads(k0))])

    meta_mask = lax.broadcasted_iota(jnp.int32, (ATT_TILE, BLOCK), 1) >= PAD
    tr = lax.broadcasted_iota(jnp.int32, (ATT_TILE, ATT_TILE), 0)
    tc = lax.broadcasted_iota(jnp.int32, (ATT_TILE, ATT_TILE), 1)
    diag_mask = tc <= tr

    def q_tile(j, carry):
        rows = pl.ds(pl.multiple_of(BLOCK + j * ATT_TILE, BLOCK), ATT_TILE)
        qs = heads(q_ref[rows, :])
        states = tuple(first(qh, kh, v0, meta_mask) for qh, kh in zip(qs, heads(k0)))

        def kv_chunk(ci, st):
            kr = pl.ds(pl.multiple_of(BLOCK + ci * ATT_TILE, BLOCK), ATT_TILE)
            ks, vc = heads(k_ref[kr, :]), v_ref[kr, :]
            return tuple(update(s, qh, kh, vc, None) for s, qh, kh in zip(st, qs, ks))

        states = lax.fori_loop(0, j, kv_chunk, states)
        ks, vc = heads(k_ref[rows, :]), v_ref[rows, :]
        states = [update(s, qh, kh, vc, diag_mask) for s, qh, kh in zip(states, qs, ks)]
        o_ref[rows, :] = finish(states)
        return carry

    lax.fori_loop(0, n_tiles, q_tile, 0)


def _mla(q, k, v):
    bsz, lp, _ = q.shape
    n_pairs = MLA_HEADS // 2
    n_tiles = (lp - BLOCK) // ATT_TILE
    qk_spec = pl.BlockSpec((None, lp, 2 * HEAD_PAD), lambda bi, p: (bi, 0, p))
    v_spec = pl.BlockSpec((None, lp, 2 * MLA_V), lambda bi, p: (bi, 0, p))
    return pl.pallas_call(
        functools.partial(_mla_kernel, n_tiles=n_tiles),
        grid=(bsz, n_pairs),
        in_specs=[qk_spec, qk_spec, v_spec],
        out_specs=v_spec,
        out_shape=jax.ShapeDtypeStruct((bsz, lp, MIX_A), BF16),
        name="mla",
    )(q, k, v)


def _swa_kernel(sinks_ref, q_ref, kp_ref, kc_ref, vp_ref, vc_ref, o_ref):
    n = pl.program_id(1)
    row = lax.broadcasted_iota(jnp.int32, (BLOCK, 2 * BLOCK), 0)
    col = lax.broadcasted_iota(jnp.int32, (BLOCK, 2 * BLOCK), 1)
    dist = BLOCK + row - col
    valid = (dist >= 0) & (dist < WINDOW) & ((n - 1) * BLOCK + col >= PAD)
    dist_f = dist.astype(F32)
    low = lax.broadcasted_iota(jnp.int32, (BLOCK, LANES), 1) < SWA_HEAD_DIM
    zero = jnp.zeros((BLOCK, LANES), BF16)

    for kvh in range(SWA_KV_HEADS):
        sl = slice(kvh * LANES, (kvh + 1) * LANES)
        k_band = jnp.concatenate([kp_ref[:, sl], kc_ref[:, sl]], axis=0)
        v_band = jnp.concatenate([vp_ref[:, sl], vc_ref[:, sl]], axis=0)
        q_rows = []
        for pair in range(SWA_GROUP // 2):
            g = kvh * (SWA_GROUP // 2) + pair
            qp = q_ref[:, g * LANES:(g + 1) * LANES]
            q_rows += [jnp.where(low, qp, zero), jnp.where(low, zero, qp)]
        s_all = _dot_nt(jnp.concatenate(q_rows, axis=0), k_band)
        probs = []
        for gi in range(SWA_GROUP):
            head = kvh * SWA_GROUP + gi
            slope = 2.0 ** (-8.0 * (head + 1) / SWA_HEADS)
            sink = sinks_ref[head]
            s = jnp.where(valid, s_all[gi * BLOCK:(gi + 1) * BLOCK] - slope * dist_f, NEG)
            m = jnp.maximum(jnp.max(s, -1, keepdims=True), sink)
            e = jnp.exp(s - m)
            p = e / (jnp.sum(e, -1, keepdims=True) + jnp.exp(sink - m))
            probs.append(p.astype(BF16))
        o_all = _dot(jnp.concatenate(probs, axis=0), v_band)
        for pair in range(SWA_GROUP // 2):
            g = kvh * (SWA_GROUP // 2) + pair
            o_lo = o_all[(2 * pair) * BLOCK:(2 * pair + 1) * BLOCK]
            o_hi = o_all[(2 * pair + 1) * BLOCK:(2 * pair + 2) * BLOCK]
            o_ref[:, g * LANES:(g + 1) * LANES] = jnp.where(low, o_lo, o_hi).astype(BF16)


def _swa(sinks, sq, skd, svd):
    bsz, lp, _ = sq.shape
    nb = lp // BLOCK
    cur = lambda w: pl.BlockSpec((None, BLOCK, w), lambda bi, i: (bi, i, 0))
    prev = lambda w: pl.BlockSpec((None, BLOCK, w), lambda bi, i: (bi, jnp.maximum(i - 1, 0), 0))
    return pl.pallas_call(
        _swa_kernel,
        grid=(bsz, nb),
        in_specs=[pl.BlockSpec(memory_space=pltpu.SMEM), cur(MIX_B),
                  prev(2 * LANES), cur(2 * LANES), prev(2 * LANES), cur(2 * LANES)],
        out_specs=cur(MIX_B),
        out_shape=jax.ShapeDtypeStruct((bsz, lp, MIX_B), BF16),
        name="swa",
    )(sinks, sq, skd, skd, svd, svd)


def _mix_kernel(oa_ref, ob_ref, h_ref, ga_ref, gb_ref, w_ref, g_ref, b_ref, o_ref):
    a = _rms_norm(oa_ref[...].astype(F32), ga_ref[...]).astype(BF16)
    b = _rms_norm(ob_ref[...].astype(F32), gb_ref[...]).astype(BF16)
    y = _dot(jnp.concatenate([a, b], axis=-1), w_ref[...])
    o_ref[...] = _layer_norm(ALPHA * h_ref[...] + y, g_ref[...], b_ref[...])


def _mix(o_a, o_b, h, ga, gb, w_out, g, b, tm):
    bsz, lp, d = h.shape
    row = lambda w: pl.BlockSpec((None, tm, w), lambda bi, i: (bi, i, 0))
    full = lambda a: pl.BlockSpec(a.shape, lambda bi, i: (0,) * a.ndim)
    return pl.pallas_call(
        _mix_kernel,
        grid=(bsz, lp // tm),
        in_specs=[row(MIX_A), row(MIX_B), row(d), full(ga), full(gb), full(w_out), full(g), full(b)],
        out_specs=row(d),
        out_shape=jax.ShapeDtypeStruct((bsz, lp, d), F32),
        name="mix",
    )(o_a, o_b, h, ga, gb, w_out, g, b)


def _ffn_kernel(h_ref, wg_ref, wu_ref, wd_ref, g_ref, b_ref, o_ref):
    x = h_ref[...]
    xb = x.astype(BF16)
    gate = _dot(xb, wg_ref[...])
    up = _dot(xb, wu_ref[...])
    act = (gate * (1.0 / (1.0 + jnp.exp(-gate))) * up).astype(BF16)
    f = _dot(act, wd_ref[...])
    o_ref[...] = _layer_norm(ALPHA * x + f, g_ref[...], b_ref[...])


def _ffn(h, wg, wu, wd, g, b, tm):
    bsz, lp, d = h.shape
    row = pl.BlockSpec((None, tm, d), lambda bi, i: (bi, i, 0))
    full = lambda a: pl.BlockSpec(a.shape, lambda bi, i: (0,) * a.ndim)
    return pl.pallas_call(
        _ffn_kernel,
        grid=(bsz, lp // tm),
        in_specs=[row, full(wg), full(wu), full(wd), full(g), full(b)],
        out_specs=row,
        out_shape=jax.ShapeDtypeStruct((bsz, lp, d), F32),
        compiler_params=pltpu.CompilerParams(vmem_limit_bytes=60 * 1024 * 1024),
        name="ffn",
    )(h, wg, wu, wd, g, b)


def _rope_tables(lp):
    pos = jnp.maximum(jnp.arange(lp, dtype=jnp.int32) - PAD, 0).astype(F32)
    inv = ROPE_THETA ** (-jnp.arange(0, MLA_ROPE, 2, dtype=F32) / MLA_ROPE)
    ang = pos[:, None] * inv[None, :]
    cos, sin = jnp.cos(ang), jnp.sin(ang)
    z16 = jnp.zeros((lp, ROPE_HALF), F32)
    z32 = jnp.zeros((lp, LANES - MLA_NOPE - MLA_ROPE), F32)
    one = jnp.ones((lp, MLA_NOPE), F32)
    z64 = jnp.zeros((lp, MLA_NOPE), F32)
    c = jnp.concatenate([one, cos, cos, z32], -1)
    sa = jnp.concatenate([z64, -sin, z16, z32], -1)
    sb = jnp.concatenate([z64, z16, sin, z32], -1)
    return c, sa, sb


def _layout_w_in(w):
    d = w.shape[0]
    o = 0
    q_c = w[:, o:o + MLA_Q_RANK]; o += MLA_Q_RANK
    kv_c = w[:, o:o + MLA_KV_RANK]; o += MLA_KV_RANK
    k_r = w[:, o:o + MLA_ROPE]; o += MLA_ROPE
    rest = w[:, o:]
    k_r_p = jnp.concatenate([jnp.zeros((d, MLA_NOPE), w.dtype), k_r,
                             jnp.zeros((d, LANES - MLA_NOPE - MLA_ROPE), w.dtype)], -1)
    return jnp.concatenate([q_c, kv_c, k_r_p, rest], -1).astype(BF16)


def _layout_w_uq(w):
    w = w.reshape(MLA_Q_RANK, MLA_HEADS, MLA_NOPE + MLA_ROPE)
    w = jnp.pad(w, ((0, 0), (0, 0), (0, HEAD_PAD - MLA_NOPE - MLA_ROPE)))
    return w.reshape(MLA_Q_RANK, MLA_HEADS * HEAD_PAD).astype(BF16)


def _layout_w_ukv(w):
    w = w.reshape(MLA_KV_RANK, MLA_HEADS, MLA_NOPE + MLA_V)
    k = jnp.pad(w[..., :MLA_NOPE], ((0, 0), (0, 0), (0, HEAD_PAD - MLA_NOPE)))
    v = w[..., MLA_NOPE:]
    return jnp.concatenate([k.reshape(MLA_KV_RANK, -1), v.reshape(MLA_KV_RANK, -1)], -1).astype(BF16)


def kernel(x, meta_tokens, ln_in_g, ln_in_b, w_in, q_norm_g, w_uq, kv_norm_g, w_ukv, attn_sinks,
           grp_norm_a, grp_norm_b, w_out, ln1_g, ln1_b, w_gate_up, w_down, ln2_g, ln2_b):
    bsz, s, d = x.shape
    assert d == D_MODEL and s % ATT_TILE == 0
    lp = s + BLOCK
    tm = _row_tile(lp)
    vec = lambda a: a.reshape(1, -1)

    meta_pad = jnp.pad(meta_tokens.astype(x.dtype), ((PAD, 0), (0, 0)))
    h = _ln_in(x, meta_pad, vec(ln_in_g), vec(ln_in_b))
    c, sa, sb = _rope_tables(lp)

    for l in range(DEPTH):
        q, k, v, sq, skd, svd = _proj(
            h, _layout_w_in(w_in[l]), vec(q_norm_g[l]), _layout_w_uq(w_uq[l]),
            vec(kv_norm_g[l]), _layout_w_ukv(w_ukv[l]), c, sa, sb, tm)
        o_a = _mla(q, k, v)
        o_b = _swa(attn_sinks[l], sq, skd, svd)
        h = _mix(o_a, o_b, h, vec(grp_norm_a[l]), vec(grp_norm_b[l]), w_out[l].astype(BF16),
                 vec(ln1_g[l]), vec(ln1_b[l]), tm)
        h = _ffn(h, w_gate_up[l, :, :D_FF].astype(BF16), w_gate_up[l, :, D_FF:].astype(BF16),
                 w_down[l].astype(BF16), vec(ln2_g[l]), vec(ln2_b[l]), tm)
    return h[:, BLOCK:]
```

```python
import functools

import jax
import jax.numpy as jnp
import numpy as np
from jax import lax
from jax.experimental import pallas as pl
from jax.experimental.pallas import tpu as pltpu

D_MODEL = 1024
DEPTH = 4
N_META = 16
BLOCK = 128
PAD = BLOCK - N_META
MLA_HEADS = 8
MLA_Q_RANK = 256
MLA_KV_RANK = 128
MLA_NOPE = 64
MLA_ROPE = 32
MLA_V = 64
ROPE_THETA = 10000.0
SWA_HEADS = 8
SWA_KV_HEADS = 2
SWA_GROUP = SWA_HEADS // SWA_KV_HEADS
SWA_HEAD_DIM = 64
WINDOW = 128
MIX_A = MLA_HEADS * MLA_V
MIX_B = SWA_HEADS * SWA_HEAD_DIM
D_FF = 2816
ALPHA = (2 * DEPTH) ** 0.25
LN_EPS = 1e-5
RMS_EPS = 1e-6
NEG = -1e30

LANES = 128
HEAD_PAD = LANES
ROPE_HALF = MLA_ROPE // 2
ATT_TILE = 512
IN_COLS_P = MLA_Q_RANK + LANES + LANES + MIX_B + LANES + LANES

BF16 = jnp.bfloat16
F32 = jnp.float32


def _row_tile(lp):
    for t in (512, 384, 256, 128):
        if lp % t == 0:
            return t
    raise ValueError(f"padded length {lp} is not a multiple of {BLOCK}")


def _layer_norm(z, g, b):
    mu = jnp.mean(z, -1, keepdims=True)
    zc = z - mu
    var = jnp.mean(zc * zc, -1, keepdims=True)
    return zc * lax.rsqrt(var + LN_EPS) * g + b


def _rms_norm(z, g):
    return z * lax.rsqrt(jnp.mean(z * z, -1, keepdims=True) + RMS_EPS) * g


def _dot(a, b):
    return jnp.dot(a, b, preferred_element_type=F32)


def _dot_nt(a, b):
    return lax.dot_general(a, b, (((1,), (1,)), ((), ())), preferred_element_type=F32)


def _ln_in_kernel(*refs, n_sub):
    x_refs, (meta_ref, g_ref, b_ref, o_ref) = refs[:n_sub], refs[n_sub:]
    i = pl.program_id(1)
    for t, x_ref in enumerate(x_refs):
        z = x_ref[...]
        if t == 0:
            z = jnp.where(i == 0, meta_ref[...], z)
        o_ref[t * BLOCK:(t + 1) * BLOCK, :] = _layer_norm(z, g_ref[...], b_ref[...])


def _ln_in(x, meta_pad, g, b, tm):
    bsz, s, d = x.shape
    lp = s + BLOCK
    n_sub = tm // BLOCK
    x_spec = lambda t: pl.BlockSpec((None, BLOCK, d), lambda bi, i: (bi, jnp.maximum(i * n_sub + t - 1, 0), 0))
    full = lambda a: pl.BlockSpec(a.shape, lambda bi, i: (0,) * a.ndim)
    return pl.pallas_call(
        functools.partial(_ln_in_kernel, n_sub=n_sub),
        grid=(bsz, lp // tm),
        in_specs=[x_spec(t) for t in range(n_sub)] + [full(meta_pad), full(g), full(b)],
        out_specs=pl.BlockSpec((None, tm, d), lambda bi, i: (bi, i, 0)),
        out_shape=jax.ShapeDtypeStruct((bsz, lp, d), F32),
        name="ln_in",
    )(*([x] * n_sub), meta_pad, g, b)


def _rope(z, c, sa, sb):
    return z * c + pltpu.roll(z, LANES - ROPE_HALF, 1) * sa + pltpu.roll(z, ROPE_HALF, 1) * sb


def _proj_kernel(h_ref, w_in_ref, gq_ref, w_uq_ref, gkv_ref, w_ukv_ref, c_ref, sa_ref, sb_ref,
                 q_ref, k_ref, v_ref, sq_ref, sk_ref, sv_ref):
    x = h_ref[...].astype(BF16)
    proj = _dot(x, w_in_ref[...])
    c, sa, sb = c_ref[...], sa_ref[...], sb_ref[...]
    o = 0
    q_c = proj[:, o:o + MLA_Q_RANK]; o += MLA_Q_RANK
    kv_c = proj[:, o:o + MLA_KV_RANK]; o += MLA_KV_RANK
    k_r = proj[:, o:o + LANES]; o += LANES
    s_q = proj[:, o:o + MIX_B]; o += MIX_B
    s_k = proj[:, o:o + LANES]; o += LANES
    s_v = proj[:, o:o + LANES]

    q = _dot(_rms_norm(q_c, gq_ref[...]).astype(BF16), w_uq_ref[...])
    q_scale = float((MLA_NOPE + MLA_ROPE) ** -0.5 * np.log2(np.e))
    for h in range(MLA_HEADS):
        sl = slice(h * HEAD_PAD, (h + 1) * HEAD_PAD)
        q_ref[:, sl] = (_rope(q[:, sl], c, sa, sb) * q_scale).astype(BF16)

    kv = _dot(_rms_norm(kv_c, gkv_ref[...]).astype(BF16), w_ukv_ref[...])
    k_rope = _rope(k_r, c, sa, sb)
    for h in range(MLA_HEADS):
        sl = slice(h * HEAD_PAD, (h + 1) * HEAD_PAD)
        k_ref[:, sl] = (kv[:, sl] + k_rope).astype(BF16)
    v_ref[...] = kv[:, MLA_HEADS * HEAD_PAD:].astype(BF16)

    sq_ref[...] = (s_q * (SWA_HEAD_DIM ** -0.5)).astype(BF16)
    low = lax.broadcasted_iota(jnp.int32, s_k.shape, 1) < SWA_HEAD_DIM
    for src, dst in ((s_k, sk_ref), (s_v, sv_ref)):
        swapped = pltpu.roll(src, SWA_HEAD_DIM, 1)
        dst[:, :LANES] = jnp.where(low, src, swapped).astype(BF16)
        dst[:, LANES:] = jnp.where(low, swapped, src).astype(BF16)


def _proj(h, w_in_p, gq, w_uq_p, gkv, w_ukv_p, c, sa, sb, tm):
    bsz, lp, d = h.shape
    nt = lp // tm
    row = lambda w: pl.BlockSpec((None, tm, w), lambda bi, i: (bi, i, 0))
    full = lambda a: pl.BlockSpec(a.shape, lambda bi, i: (0,) * a.ndim)
    tab = pl.BlockSpec((tm, LANES), lambda bi, i: (i, 0))
    widths = (MLA_HEADS * HEAD_PAD, MLA_HEADS * HEAD_PAD, MIX_A, MIX_B, 2 * LANES, 2 * LANES)
    return pl.pallas_call(
        _proj_kernel,
        grid=(bsz, nt),
        in_specs=[row(d), full(w_in_p), full(gq), full(w_uq_p), full(gkv), full(w_ukv_p), tab, tab, tab],
        out_specs=[row(w) for w in widths],
        out_shape=[jax.ShapeDtypeStruct((bsz, lp, w), BF16) for w in widths],
        name="proj",
    )(h, w_in_p, gq, w_uq_p, gkv, w_ukv_p, c, sa, sb)


def _mla_kernel(q_ref, k_ref, v_ref, o_ref, acc_ref, *, n_tiles):
    lane = lax.broadcasted_iota(jnp.int32, (1, 2 * LANES), 1)
    head_a = (lane % LANES) < MLA_V

    def value_matrix(vc):
        n = vc.shape[0]
        lane_i = lax.broadcasted_iota(jnp.int32, (n, LANES), 1)
        low = lane_i < MLA_V
        zero = jnp.zeros((n, LANES), BF16)
        one_lo = jnp.clip(MLA_V - lane_i, 0, 1).astype(F32)
        top = jnp.concatenate([jnp.where(low, vc, zero), one_lo.astype(BF16)], axis=1)
        bot = jnp.concatenate([jnp.where(low, zero, vc), (1.0 - one_lo).astype(BF16)], axis=1)
        return jnp.concatenate([top, bot], axis=0)

    def heads(x):
        return x[:, :HEAD_PAD], x[:, HEAD_PAD:]

    def scores(qs, kc, mask):
        out = []
        for qh, kh in zip(qs, heads(kc)):
            s = _dot_nt(qh, kh)
            out.append(s if mask is None else jnp.where(mask, s, NEG))
        return out

    def first(qs, kc, vc, mask):
        ss = scores(qs, kc, mask)
        ms = [jnp.max(s, -1, keepdims=True) for s in ss]
        p = jnp.concatenate([jnp.exp2(s - m).astype(BF16) for s, m in zip(ss, ms)], axis=1)
        return ms, _dot(p, value_matrix(vc))

    def softmax_step(ms, ss):
        ms_new = [jnp.maximum(m, jnp.max(s, -1, keepdims=True)) for s, m in zip(ss, ms)]
        alpha = [jnp.exp2(m - mn) for m, mn in zip(ms, ms_new)]
        p = jnp.concatenate([jnp.exp2(s - mn).astype(BF16) for s, mn in zip(ss, ms_new)], axis=1)
        return tuple(ms_new), tuple(alpha), p

    def accumulate(alpha, p, vc):
        acc_ref[...] = jnp.where(head_a, alpha[0], alpha[1]) * acc_ref[...] + _dot(p, value_matrix(vc))

    def normalized(acc):
        return (acc[:, :LANES] / acc[:, LANES:]).astype(BF16)

    r = lax.broadcasted_iota(jnp.int32, (BLOCK, BLOCK), 0)
    cidx = lax.broadcasted_iota(jnp.int32, (BLOCK, BLOCK), 1)
    k0, v0 = k_ref[:BLOCK, :], v_ref[:BLOCK, :]
    _, acc0 = first(heads(q_ref[:BLOCK, :]), k0, v0, (cidx <= r) & (cidx >= PAD))
    o_ref[:BLOCK, :] = normalized(acc0)

    meta_mask = lax.broadcasted_iota(jnp.int32, (ATT_TILE, BLOCK), 1) >= PAD
    tr = lax.broadcasted_iota(jnp.int32, (ATT_TILE, ATT_TILE), 0)
    tc = lax.broadcasted_iota(jnp.int32, (ATT_TILE, ATT_TILE), 1)
    diag_mask = tc <= tr

    def q_tile(j, carry):
        rows = pl.ds(pl.multiple_of(BLOCK + j * ATT_TILE, BLOCK), ATT_TILE)
        qs = heads(q_ref[rows, :])
        ms, acc_first = first(qs, k0, v0, meta_mask)
        acc_ref[...] = acc_first

        def update(ms, kc, vc, mask):
            ms, alpha, p = softmax_step(ms, scores(qs, kc, mask))
            accumulate(alpha, p, vc)
            return ms

        def kv_chunk(ci, ms):
            kr = pl.ds(pl.multiple_of(BLOCK + ci * ATT_TILE, BLOCK), ATT_TILE)
            return update(ms, k_ref[kr, :], v_ref[kr, :], None)

        ms = lax.fori_loop(0, j, kv_chunk, tuple(ms))
        update(ms, k_ref[rows, :], v_ref[rows, :], diag_mask)
        o_ref[rows, :] = normalized(acc_ref[...])
        return carry

    lax.fori_loop(0, n_tiles, q_tile, 0)


def _mla(q, k, v):
    bsz, lp, _ = q.shape
    n_pairs = MLA_HEADS // 2
    n_tiles = (lp - BLOCK) // ATT_TILE
    qk_spec = pl.BlockSpec((None, lp, 2 * HEAD_PAD), lambda bi, p: (bi, 0, p))
    v_spec = pl.BlockSpec((None, lp, 2 * MLA_V), lambda bi, p: (bi, 0, p))
    return pl.pallas_call(
        functools.partial(_mla_kernel, n_tiles=n_tiles),
        grid=(bsz, n_pairs),
        in_specs=[qk_spec, qk_spec, v_spec],
        out_specs=v_spec,
        out_shape=jax.ShapeDtypeStruct((bsz, lp, MIX_A), BF16),
        scratch_shapes=[pltpu.VMEM((ATT_TILE, 2 * LANES), F32)],
        name="mla",
    )(q, k, v)


def _swa_kernel(sinks_ref, q_ref, kp_ref, kc_ref, vp_ref, vc_ref, o_ref, *, n_sub):
    first_block = pl.program_id(1) * n_sub
    row = lax.broadcasted_iota(jnp.int32, (BLOCK, 2 * BLOCK), 0)
    col = lax.broadcasted_iota(jnp.int32, (BLOCK, 2 * BLOCK), 1)
    dist = BLOCK + row - col
    in_band = (dist >= 0) & (dist < WINDOW)
    dist_f = dist.astype(F32)
    low = lax.broadcasted_iota(jnp.int32, (BLOCK, LANES), 1) < SWA_HEAD_DIM
    zero = jnp.zeros((BLOCK, LANES), BF16)

    for t in range(n_sub):
        rows = slice(t * BLOCK, (t + 1) * BLOCK)
        valid = in_band & ((first_block + t - 1) * BLOCK + col >= PAD)
        for kvh in range(SWA_KV_HEADS):
            sl = slice(kvh * LANES, (kvh + 1) * LANES)
            if t == 0:
                k_band = jnp.concatenate([kp_ref[:, sl], kc_ref[rows, sl]], axis=0)
                v_band = jnp.concatenate([vp_ref[:, sl], vc_ref[rows, sl]], axis=0)
            else:
                band = slice((t - 1) * BLOCK, (t + 1) * BLOCK)
                k_band, v_band = kc_ref[band, sl], vc_ref[band, sl]
            q_rows = []
            for pair in range(SWA_GROUP // 2):
                g = kvh * (SWA_GROUP // 2) + pair
                qp = q_ref[rows, g * LANES:(g + 1) * LANES]
                q_rows += [jnp.where(low, qp, zero), jnp.where(low, zero, qp)]
            s_all = _dot_nt(jnp.concatenate(q_rows, axis=0), k_band)
            probs = []
            for gi in range(SWA_GROUP):
                head = kvh * SWA_GROUP + gi
                slope = 2.0 ** (-8.0 * (head + 1) / SWA_HEADS)
                sink = sinks_ref[head]
                s = jnp.where(valid, s_all[gi * BLOCK:(gi + 1) * BLOCK] - slope * dist_f, NEG)
                m = jnp.maximum(jnp.max(s, -1, keepdims=True), sink)
                e = jnp.exp(s - m)
                p = e / (jnp.sum(e, -1, keepdims=True) + jnp.exp(sink - m))
                probs.append(p.astype(BF16))
            o_all = _dot(jnp.concatenate(probs, axis=0), v_band)
            for pair in range(SWA_GROUP // 2):
                g = kvh * (SWA_GROUP // 2) + pair
                o_lo = o_all[(2 * pair) * BLOCK:(2 * pair + 1) * BLOCK]
                o_hi = o_all[(2 * pair + 1) * BLOCK:(2 * pair + 2) * BLOCK]
                o_ref[rows, g * LANES:(g + 1) * LANES] = jnp.where(low, o_lo, o_hi).astype(BF16)


def _swa(sinks, sq, skd, svd, tm):
    bsz, lp, _ = sq.shape
    n_sub = tm // BLOCK
    cur = lambda w: pl.BlockSpec((None, tm, w), lambda bi, i: (bi, i, 0))
    prev = lambda w: pl.BlockSpec((None, BLOCK, w), lambda bi, i: (bi, jnp.maximum(i * n_sub - 1, 0), 0))
    return pl.pallas_call(
        functools.partial(_swa_kernel, n_sub=n_sub),
        grid=(bsz, lp // tm),
        in_specs=[pl.BlockSpec(memory_space=pltpu.SMEM), cur(MIX_B),
                  prev(2 * LANES), cur(2 * LANES), prev(2 * LANES), cur(2 * LANES)],
        out_specs=cur(MIX_B),
        out_shape=jax.ShapeDtypeStruct((bsz, lp, MIX_B), BF16),
        name="swa",
    )(sinks, sq, skd, skd, svd, svd)


def _mix_kernel(oa_ref, ob_ref, h_ref, ga_ref, gb_ref, w_ref, g_ref, b_ref, o_ref):
    a = _rms_norm(oa_ref[...].astype(F32), ga_ref[...]).astype(BF16)
    b = _rms_norm(ob_ref[...].astype(F32), gb_ref[...]).astype(BF16)
    y = _dot(jnp.concatenate([a, b], axis=-1), w_ref[...])
    o_ref[...] = _layer_norm(ALPHA * h_ref[...] + y, g_ref[...], b_ref[...])


def _mix(o_a, o_b, h, ga, gb, w_out, g, b, tm):
    bsz, lp, d = h.shape
    row = lambda w: pl.BlockSpec((None, tm, w), lambda bi, i: (bi, i, 0))
    full = lambda a: pl.BlockSpec(a.shape, lambda bi, i: (0,) * a.ndim)
    return pl.pallas_call(
        _mix_kernel,
        grid=(bsz, lp // tm),
        in_specs=[row(MIX_A), row(MIX_B), row(d), full(ga), full(gb), full(w_out), full(g), full(b)],
        out_specs=row(d),
        out_shape=jax.ShapeDtypeStruct((bsz, lp, d), F32),
        name="mix",
    )(o_a, o_b, h, ga, gb, w_out, g, b)


def _ffn_kernel(h_ref, wg_ref, wu_ref, wd_ref, g_ref, b_ref, o_ref):
    x = h_ref[...]
    xb = x.astype(BF16)
    gate = _dot(xb, wg_ref[...])
    up = _dot(xb, wu_ref[...])
    act = (gate * (1.0 / (1.0 + jnp.exp(-gate))) * up).astype(BF16)
    f = _dot(act, wd_ref[...])
    o_ref[...] = _layer_norm(ALPHA * x + f, g_ref[...], b_ref[...])


def _ffn(h, wg, wu, wd, g, b, tm):
    bsz, lp, d = h.shape
    row = pl.BlockSpec((None, tm, d), lambda bi, i: (bi, i, 0))
    full = lambda a: pl.BlockSpec(a.shape, lambda bi, i: (0,) * a.ndim)
    return pl.pallas_call(
        _ffn_kernel,
        grid=(bsz, lp // tm),
        in_specs=[row, full(wg), full(wu), full(wd), full(g), full(b)],
        out_specs=row,
        out_shape=jax.ShapeDtypeStruct((bsz, lp, d), F32),
        compiler_params=pltpu.CompilerParams(vmem_limit_bytes=60 * 1024 * 1024),
        name="ffn",
    )(h, wg, wu, wd, g, b)


def _rope_tables(lp):
    pos = jnp.maximum(jnp.arange(lp, dtype=jnp.int32) - PAD, 0).astype(F32)
    inv = ROPE_THETA ** (-jnp.arange(0, MLA_ROPE, 2, dtype=F32) / MLA_ROPE)
    ang = pos[:, None] * inv[None, :]
    cos, sin = jnp.cos(ang), jnp.sin(ang)
    z16 = jnp.zeros((lp, ROPE_HALF), F32)
    z32 = jnp.zeros((lp, LANES - MLA_NOPE - MLA_ROPE), F32)
    one = jnp.ones((lp, MLA_NOPE), F32)
    z64 = jnp.zeros((lp, MLA_NOPE), F32)
    c = jnp.concatenate([one, cos, cos, z32], -1)
    sa = jnp.concatenate([z64, -sin, z16, z32], -1)
    sb = jnp.concatenate([z64, z16, sin, z32], -1)
    return c, sa, sb


def _layout_w_in(w):
    d = w.shape[0]
    o = 0
    q_c = w[:, o:o + MLA_Q_RANK]; o += MLA_Q_RANK
    kv_c = w[:, o:o + MLA_KV_RANK]; o += MLA_KV_RANK
    k_r = w[:, o:o + MLA_ROPE]; o += MLA_ROPE
    rest = w[:, o:]
    k_r_p = jnp.concatenate([jnp.zeros((d, MLA_NOPE), w.dtype), k_r,
                             jnp.zeros((d, LANES - MLA_NOPE - MLA_ROPE), w.dtype)], -1)
    return jnp.concatenate([q_c, kv_c, k_r_p, rest], -1).astype(BF16)


def _layout_w_uq(w):
    w = w.reshape(MLA_Q_RANK, MLA_HEADS, MLA_NOPE + MLA_ROPE)
    w = jnp.pad(w, ((0, 0), (0, 0), (0, HEAD_PAD - MLA_NOPE - MLA_ROPE)))
    return w.reshape(MLA_Q_RANK, MLA_HEADS * HEAD_PAD).astype(BF16)


def _layout_w_ukv(w):
    w = w.reshape(MLA_KV_RANK, MLA_HEADS, MLA_NOPE + MLA_V)
    k = jnp.pad(w[..., :MLA_NOPE], ((0, 0), (0, 0), (0, HEAD_PAD - MLA_NOPE)))
    v = w[..., MLA_NOPE:]
    return jnp.concatenate([k.reshape(MLA_KV_RANK, -1), v.reshape(MLA_KV_RANK, -1)], -1).astype(BF16)


def kernel(x, meta_tokens, ln_in_g, ln_in_b, w_in, q_norm_g, w_uq, kv_norm_g, w_ukv, attn_sinks,
           grp_norm_a, grp_norm_b, w_out, ln1_g, ln1_b, w_gate_up, w_down, ln2_g, ln2_b):
    bsz, s, d = x.shape
    assert d == D_MODEL and s % ATT_TILE == 0
    lp = s + BLOCK
    tm = _row_tile(lp)
    vec = lambda a: a.reshape(1, -1)

    meta_pad = jnp.pad(meta_tokens.astype(x.dtype), ((PAD, 0), (0, 0)))
    h = _ln_in(x, meta_pad, vec(ln_in_g), vec(ln_in_b), tm)
    c, sa, sb = _rope_tables(lp)

    for l in range(DEPTH):
        q, k, v, sq, skd, svd = _proj(
            h, _layout_w_in(w_in[l]), vec(q_norm_g[l]), _layout_w_uq(w_uq[l]),
            vec(kv_norm_g[l]), _layout_w_ukv(w_ukv[l]), c, sa, sb, tm)
        o_a = _mla(q, k, v)
        o_b = _swa(attn_sinks[l], sq, skd, svd, tm)
        h = _mix(o_a, o_b, h, vec(grp_norm_a[l]), vec(grp_norm_b[l]), w_out[l].astype(BF16),
                 vec(ln1_g[l]), vec(ln1_b[l]), tm)
        h = _ffn(h, w_gate_up[l, :, :D_FF].astype(BF16), w_gate_up[l, :, D_FF:].astype(BF16),
                 w_down[l].astype(BF16), vec(ln2_g[l]), vec(ln2_b[l]), tm)
    return h[:, BLOCK:]
```

```python
import functools

import jax
import jax.numpy as jnp
import numpy as np
from jax import lax
from jax.experimental import pallas as pl
from jax.experimental.pallas import tpu as pltpu

D_MODEL = 1024
DEPTH = 4
N_META = 16
BLOCK = 128
PAD = BLOCK - N_META
MLA_HEADS = 8
MLA_Q_RANK = 256
MLA_KV_RANK = 128
MLA_NOPE = 64
MLA_ROPE = 32
MLA_V = 64
ROPE_THETA = 10000.0
SWA_HEADS = 8
SWA_KV_HEADS = 2
SWA_GROUP = SWA_HEADS // SWA_KV_HEADS
SWA_HEAD_DIM = 64
WINDOW = 128
MIX_A = MLA_HEADS * MLA_V
MIX_B = SWA_HEADS * SWA_HEAD_DIM
D_FF = 2816
ALPHA = (2 * DEPTH) ** 0.25
LN_EPS = 1e-5
RMS_EPS = 1e-6
NEG = -1e30

LANES = 128
HEAD_PAD = LANES
ROPE_HALF = MLA_ROPE // 2
ATT_TILE = 512
IN_COLS_P = MLA_Q_RANK + LANES + LANES + MIX_B + LANES + LANES

BF16 = jnp.bfloat16
F32 = jnp.float32


def _row_tile(lp):
    for t in (512, 384, 256, 128):
        if lp % t == 0:
            return t
    raise ValueError(f"padded length {lp} is not a multiple of {BLOCK}")


def _layer_norm(z, g, b):
    mu = jnp.mean(z, -1, keepdims=True)
    zc = z - mu
    var = jnp.mean(zc * zc, -1, keepdims=True)
    return zc * lax.rsqrt(var + LN_EPS) * g + b


def _rms_norm(z, g):
    return z * lax.rsqrt(jnp.mean(z * z, -1, keepdims=True) + RMS_EPS) * g


def _dot(a, b):
    return jnp.dot(a, b, preferred_element_type=F32)


def _dot_nt(a, b):
    return lax.dot_general(a, b, (((1,), (1,)), ((), ())), preferred_element_type=F32)


def _ln_in_kernel(*refs, n_sub):
    x_refs, (meta_ref, g_ref, b_ref, o_ref) = refs[:n_sub], refs[n_sub:]
    i = pl.program_id(1)
    for t, x_ref in enumerate(x_refs):
        z = x_ref[...]
        if t == 0:
            z = jnp.where(i == 0, meta_ref[...], z)
        o_ref[t * BLOCK:(t + 1) * BLOCK, :] = _layer_norm(z, g_ref[...], b_ref[...])


def _ln_in(x, meta_pad, g, b, tm):
    bsz, s, d = x.shape
    lp = s + BLOCK
    n_sub = tm // BLOCK
    x_spec = lambda t: pl.BlockSpec((None, BLOCK, d), lambda bi, i: (bi, jnp.maximum(i * n_sub + t - 1, 0), 0))
    full = lambda a: pl.BlockSpec(a.shape, lambda bi, i: (0,) * a.ndim)
    return pl.pallas_call(
        functools.partial(_ln_in_kernel, n_sub=n_sub),
        grid=(bsz, lp // tm),
        in_specs=[x_spec(t) for t in range(n_sub)] + [full(meta_pad), full(g), full(b)],
        out_specs=pl.BlockSpec((None, tm, d), lambda bi, i: (bi, i, 0)),
        out_shape=jax.ShapeDtypeStruct((bsz, lp, d), F32),
        name="ln_in",
    )(*([x] * n_sub), meta_pad, g, b)


def _rope(z, c, sa, sb):
    return z * c + pltpu.roll(z, LANES - ROPE_HALF, 1) * sa + pltpu.roll(z, ROPE_HALF, 1) * sb


def _proj_kernel(h_ref, w_in_ref, gq_ref, w_uq_ref, gkv_ref, w_ukv_ref, c_ref, sa_ref, sb_ref,
                 q_ref, k_ref, v_ref, sq_ref, sk_ref, sv_ref):
    x = h_ref[...].astype(BF16)
    proj = _dot(x, w_in_ref[...])
    c, sa, sb = c_ref[...], sa_ref[...], sb_ref[...]
    o = 0
    q_c = proj[:, o:o + MLA_Q_RANK]; o += MLA_Q_RANK
    kv_c = proj[:, o:o + MLA_KV_RANK]; o += MLA_KV_RANK
    k_r = proj[:, o:o + LANES]; o += LANES
    s_q = proj[:, o:o + MIX_B]; o += MIX_B
    s_k = proj[:, o:o + LANES]; o += LANES
    s_v = proj[:, o:o + LANES]

    q = _dot(_rms_norm(q_c, gq_ref[...]).astype(BF16), w_uq_ref[...])
    q_scale = float((MLA_NOPE + MLA_ROPE) ** -0.5 * np.log2(np.e))
    for h in range(MLA_HEADS):
        sl = slice(h * HEAD_PAD, (h + 1) * HEAD_PAD)
        q_ref[:, sl] = (_rope(q[:, sl], c, sa, sb) * q_scale).astype(BF16)

    kv = _dot(_rms_norm(kv_c, gkv_ref[...]).astype(BF16), w_ukv_ref[...])
    k_rope = _rope(k_r, c, sa, sb)
    for h in range(MLA_HEADS):
        sl = slice(h * HEAD_PAD, (h + 1) * HEAD_PAD)
        k_ref[:, sl] = (kv[:, sl] + k_rope).astype(BF16)
    v_ref[...] = kv[:, MLA_HEADS * HEAD_PAD:].astype(BF16)

    sq_ref[...] = (s_q * (SWA_HEAD_DIM ** -0.5)).astype(BF16)
    low = lax.broadcasted_iota(jnp.int32, s_k.shape, 1) < SWA_HEAD_DIM
    for src, dst in ((s_k, sk_ref), (s_v, sv_ref)):
        swapped = pltpu.roll(src, SWA_HEAD_DIM, 1)
        dst[:, :LANES] = jnp.where(low, src, swapped).astype(BF16)
        dst[:, LANES:] = jnp.where(low, swapped, src).astype(BF16)


def _proj(h, w_in_p, gq, w_uq_p, gkv, w_ukv_p, c, sa, sb, tm):
    bsz, lp, d = h.shape
    nt = lp // tm
    row = lambda w: pl.BlockSpec((None, tm, w), lambda bi, i: (bi, i, 0))
    full = lambda a: pl.BlockSpec(a.shape, lambda bi, i: (0,) * a.ndim)
    tab = pl.BlockSpec((tm, LANES), lambda bi, i: (i, 0))
    widths = (MLA_HEADS * HEAD_PAD, MLA_HEADS * HEAD_PAD, MIX_A, MIX_B, 2 * LANES, 2 * LANES)
    return pl.pallas_call(
        _proj_kernel,
        grid=(bsz, nt),
        in_specs=[row(d), full(w_in_p), full(gq), full(w_uq_p), full(gkv), full(w_ukv_p), tab, tab, tab],
        out_specs=[row(w) for w in widths],
        out_shape=[jax.ShapeDtypeStruct((bsz, lp, w), BF16) for w in widths],
        name="proj",
    )(h, w_in_p, gq, w_uq_p, gkv, w_ukv_p, c, sa, sb)


def _mla_schedule(n_tiles):
    rows = []
    for j in range(n_tiles):
        for c in range(j + 1):
            rows.append((BLOCK + j * ATT_TILE, BLOCK + c * ATT_TILE, j, int(c == 0), int(c == j)))
    return np.asarray(rows, np.int32)


def _mla_kernel(sched_ref, q_ref, k_ref, v_ref, o_ref,
                s_ref, p_ref, alpha_ref, m_ref, acc_ref, meta_m_ref, meta_acc_ref, *, n_tiles):
    n_steps = n_tiles * (n_tiles + 1) // 2
    lane = lax.broadcasted_iota(jnp.int32, (1, 2 * LANES), 1)
    head_a = (lane % LANES) < MLA_V
    tr = lax.broadcasted_iota(jnp.int32, (ATT_TILE, ATT_TILE), 0)
    tc = lax.broadcasted_iota(jnp.int32, (ATT_TILE, ATT_TILE), 1)
    diag_mask = tc <= tr

    def sched(t, col):
        return sched_ref[t, col]

    def rows_at(off):
        return pl.ds(pl.multiple_of(off, BLOCK), ATT_TILE)

    def value_matrix(vc):
        n = vc.shape[0]
        lane_i = lax.broadcasted_iota(jnp.int32, (n, LANES), 1)
        low = lane_i < MLA_V
        zero = jnp.zeros((n, LANES), BF16)
        one_lo = jnp.clip(MLA_V - lane_i, 0, 1).astype(F32)
        top = jnp.concatenate([jnp.where(low, vc, zero), one_lo.astype(BF16)], axis=1)
        bot = jnp.concatenate([jnp.where(low, zero, vc), (1.0 - one_lo).astype(BF16)], axis=1)
        return jnp.concatenate([top, bot], axis=0)

    def heads(x):
        return x[:, :HEAD_PAD], x[:, HEAD_PAD:]

    def normalized(acc):
        return (acc[:, :LANES] / acc[:, LANES:]).astype(BF16)

    def meta_attention(q, mask):
        ms, ps = [], []
        for qh, kh in zip(heads(q), heads(k_ref[:BLOCK, :])):
            s = jnp.where(mask, _dot_nt(qh, kh), NEG)
            m = jnp.max(s, -1, keepdims=True)
            ms.append(m)
            ps.append(jnp.exp2(s - m).astype(BF16))
        return ms, _dot(jnp.concatenate(ps, axis=1), value_matrix(v_ref[:BLOCK, :]))

    def score_stage(t, slot):
        q, kc = q_ref[rows_at(sched(t, 0)), :], k_ref[rows_at(sched(t, 1)), :]
        for h, (qh, kh) in enumerate(zip(heads(q), heads(kc))):
            s_ref[slot, h] = _dot_nt(qh, kh)

    def mask_stage(t, slot):
        @pl.when(sched(t, 4) == 1)
        def _():
            for h in range(2):
                s_ref[slot, h] = jnp.where(diag_mask, s_ref[slot, h], NEG)

    def softmax_stage(t, slot):
        first = sched(t, 3) == 1
        alphas = []
        for h in range(2):
            s = s_ref[slot, h]
            m_prev = jnp.where(first, meta_m_ref[sched(t, 2), h], m_ref[h])
            m_new = jnp.maximum(m_prev, jnp.max(s, -1, keepdims=True))
            alphas.append(jnp.exp2(m_prev - m_new))
            m_ref[h] = m_new
            p_ref[slot, :, h * ATT_TILE:(h + 1) * ATT_TILE] = jnp.exp2(s - m_new).astype(BF16)
        alpha_ref[slot] = jnp.where(head_a, alphas[0], alphas[1])

    def value_stage(t, slot):
        first = sched(t, 3) == 1
        pv = _dot(p_ref[slot], value_matrix(v_ref[rows_at(sched(t, 1)), :]))
        base = jnp.where(first, meta_acc_ref[sched(t, 2)], acc_ref[...])
        acc_ref[...] = alpha_ref[slot] * base + pv

    def output_stage(t):
        @pl.when(sched(t, 4) == 1)
        def _():
            o_ref[rows_at(sched(t, 0)), :] = normalized(acc_ref[...])

    def step(t, slot):
        score_stage(t + 1, 1 - slot)
        value_stage(t - 1, 1 - slot)
        softmax_stage(t, slot)
        output_stage(t - 1)
        mask_stage(t + 1, 1 - slot)

    r = lax.broadcasted_iota(jnp.int32, (BLOCK, BLOCK), 0)
    cidx = lax.broadcasted_iota(jnp.int32, (BLOCK, BLOCK), 1)
    _, acc0 = meta_attention(q_ref[:BLOCK, :], (cidx <= r) & (cidx >= PAD))
    o_ref[:BLOCK, :] = normalized(acc0)

    meta_mask = lax.broadcasted_iota(jnp.int32, (ATT_TILE, BLOCK), 1) >= PAD
    for j in range(n_tiles):
        ms, acc = meta_attention(q_ref[BLOCK + j * ATT_TILE:BLOCK + (j + 1) * ATT_TILE, :], meta_mask)
        meta_acc_ref[j] = acc
        for h in range(2):
            meta_m_ref[j, h] = ms[h]

    score_stage(0, 0)
    mask_stage(0, 0)
    if n_steps == 1:
        softmax_stage(0, 0)
    else:
        score_stage(1, 1)
        softmax_stage(0, 0)
        mask_stage(1, 1)
        n_mid = n_steps - 2

        def two_steps(i, carry):
            t = 1 + 2 * i
            step(t, 1)
            step(t + 1, 0)
            return carry

        lax.fori_loop(0, n_mid // 2, two_steps, 0)
        if n_mid % 2:
            step(n_steps - 2, (n_steps - 2) % 2)
        last = n_steps - 1
        value_stage(last - 1, (last - 1) % 2)
        softmax_stage(last, last % 2)
        output_stage(last - 1)
    value_stage(n_steps - 1, (n_steps - 1) % 2)
    output_stage(n_steps - 1)


def _mla(q, k, v):
    bsz, lp, _ = q.shape
    n_pairs = MLA_HEADS // 2
    n_tiles = (lp - BLOCK) // ATT_TILE
    sched = jnp.asarray(_mla_schedule(n_tiles))
    qk_spec = pl.BlockSpec((None, lp, 2 * HEAD_PAD), lambda bi, p: (bi, 0, p))
    v_spec = pl.BlockSpec((None, lp, 2 * MLA_V), lambda bi, p: (bi, 0, p))
    return pl.pallas_call(
        functools.partial(_mla_kernel, n_tiles=n_tiles),
        grid=(bsz, n_pairs),
        in_specs=[pl.BlockSpec(memory_space=pltpu.SMEM), qk_spec, qk_spec, v_spec],
        out_specs=v_spec,
        out_shape=jax.ShapeDtypeStruct((bsz, lp, MIX_A), BF16),
        scratch_shapes=[
            pltpu.VMEM((2, 2, ATT_TILE, ATT_TILE), F32),
            pltpu.VMEM((2, ATT_TILE, 2 * ATT_TILE), BF16),
            pltpu.VMEM((2, ATT_TILE, 2 * LANES), F32),
            pltpu.VMEM((2, ATT_TILE, 1), F32),
            pltpu.VMEM((ATT_TILE, 2 * LANES), F32),
            pltpu.VMEM((n_tiles, 2, ATT_TILE, 1), F32),
            pltpu.VMEM((n_tiles, ATT_TILE, 2 * LANES), F32),
        ],
        name="mla",
    )(sched, q, k, v)


def _swa_kernel(sinks_ref, q_ref, kp_ref, kc_ref, vp_ref, vc_ref, o_ref, *, n_sub):
    first_block = pl.program_id(1) * n_sub
    row = lax.broadcasted_iota(jnp.int32, (BLOCK, 2 * BLOCK), 0)
    col = lax.broadcasted_iota(jnp.int32, (BLOCK, 2 * BLOCK), 1)
    dist = BLOCK + row - col
    in_band = (dist >= 0) & (dist < WINDOW)
    dist_f = dist.astype(F32)
    low = lax.broadcasted_iota(jnp.int32, (BLOCK, LANES), 1) < SWA_HEAD_DIM
    zero = jnp.zeros((BLOCK, LANES), BF16)

    for t in range(n_sub):
        rows = slice(t * BLOCK, (t + 1) * BLOCK)
        valid = in_band & ((first_block + t - 1) * BLOCK + col >= PAD)
        for kvh in range(SWA_KV_HEADS):
            sl = slice(kvh * LANES, (kvh + 1) * LANES)
            if t == 0:
                k_band = jnp.concatenate([kp_ref[:, sl], kc_ref[rows, sl]], axis=0)
                v_band = jnp.concatenate([vp_ref[:, sl], vc_ref[rows, sl]], axis=0)
            else:
                band = slice((t - 1) * BLOCK, (t + 1) * BLOCK)
                k_band, v_band = kc_ref[band, sl], vc_ref[band, sl]
            q_rows = []
            for pair in range(SWA_GROUP // 2):
                g = kvh * (SWA_GROUP // 2) + pair
                qp = q_ref[rows, g * LANES:(g + 1) * LANES]
                q_rows += [jnp.where(low, qp, zero), jnp.where(low, zero, qp)]
            s_all = _dot_nt(jnp.concatenate(q_rows, axis=0), k_band)
            probs = []
            for gi in range(SWA_GROUP):
                head = kvh * SWA_GROUP + gi
                slope = 2.0 ** (-8.0 * (head + 1) / SWA_HEADS)
                sink = sinks_ref[head]
                s = jnp.where(valid, s_all[gi * BLOCK:(gi + 1) * BLOCK] - slope * dist_f, NEG)
                m = jnp.maximum(jnp.max(s, -1, keepdims=True), sink)
                e = jnp.exp(s - m)
                p = e / (jnp.sum(e, -1, keepdims=True) + jnp.exp(sink - m))
                probs.append(p.astype(BF16))
            o_all = _dot(jnp.concatenate(probs, axis=0), v_band)
            for pair in range(SWA_GROUP // 2):
                g = kvh * (SWA_GROUP // 2) + pair
                o_lo = o_all[(2 * pair) * BLOCK:(2 * pair + 1) * BLOCK]
                o_hi = o_all[(2 * pair + 1) * BLOCK:(2 * pair + 2) * BLOCK]
                o_ref[rows, g * LANES:(g + 1) * LANES] = jnp.where(low, o_lo, o_hi).astype(BF16)


def _swa(sinks, sq, skd, svd, tm):
    bsz, lp, _ = sq.shape
    n_sub = tm // BLOCK
    cur = lambda w: pl.BlockSpec((None, tm, w), lambda bi, i: (bi, i, 0))
    prev = lambda w: pl.BlockSpec((None, BLOCK, w), lambda bi, i: (bi, jnp.maximum(i * n_sub - 1, 0), 0))
    return pl.pallas_call(
        functools.partial(_swa_kernel, n_sub=n_sub),
        grid=(bsz, lp // tm),
        in_specs=[pl.BlockSpec(memory_space=pltpu.SMEM), cur(MIX_B),
                  prev(2 * LANES), cur(2 * LANES), prev(2 * LANES), cur(2 * LANES)],
        out_specs=cur(MIX_B),
        out_shape=jax.ShapeDtypeStruct((bsz, lp, MIX_B), BF16),
        name="swa",
    )(sinks, sq, skd, skd, svd, svd)


def _mix_kernel(oa_ref, ob_ref, h_ref, ga_ref, gb_ref, w_ref, g_ref, b_ref, o_ref):
    a = _rms_norm(oa_ref[...].astype(F32), ga_ref[...]).astype(BF16)
    b = _rms_norm(ob_ref[...].astype(F32), gb_ref[...]).astype(BF16)
    y = _dot(jnp.concatenate([a, b], axis=-1), w_ref[...])
    o_ref[...] = _layer_norm(ALPHA * h_ref[...] + y, g_ref[...], b_ref[...])


def _mix(o_a, o_b, h, ga, gb, w_out, g, b, tm):
    bsz, lp, d = h.shape
    row = lambda w: pl.BlockSpec((None, tm, w), lambda bi, i: (bi, i, 0))
    full = lambda a: pl.BlockSpec(a.shape, lambda bi, i: (0,) * a.ndim)
    return pl.pallas_call(
        _mix_kernel,
        grid=(bsz, lp // tm),
        in_specs=[row(MIX_A), row(MIX_B), row(d), full(ga), full(gb), full(w_out), full(g), full(b)],
        out_specs=row(d),
        out_shape=jax.ShapeDtypeStruct((bsz, lp, d), F32),
        name="mix",
    )(o_a, o_b, h, ga, gb, w_out, g, b)


def _ffn_kernel(*refs, n_in):
    h_refs, (wg_ref, wu_ref, wd_ref, g_ref, b_ref, o_ref) = refs[:n_in], refs[n_in:]
    x = jnp.concatenate([r[...] for r in h_refs], axis=0) if n_in > 1 else h_refs[0][...]
    xb = x.astype(BF16)
    gate = _dot(xb, wg_ref[...])
    up = _dot(xb, wu_ref[...])
    act = (gate * (1.0 / (1.0 + jnp.exp(-gate))) * up).astype(BF16)
    f = _dot(act, wd_ref[...])
    o_ref[...] = _layer_norm(ALPHA * x + f, g_ref[...], b_ref[...])


def _ffn(h, wg, wu, wd, g, b, tm, final):
    bsz, lp, d = h.shape
    full = lambda a: pl.BlockSpec(a.shape, lambda bi, i: (0,) * a.ndim, pipeline_mode=pl.Buffered(1))
    if final:
        rows, n_in = lp - BLOCK, tm // BLOCK
        h_specs = [pl.BlockSpec((None, BLOCK, d), lambda bi, i, t=t: (bi, 1 + i * n_in + t, 0))
                   for t in range(n_in)]
    else:
        rows, n_in = lp, 1
        h_specs = [pl.BlockSpec((None, tm, d), lambda bi, i: (bi, i, 0))]
    return pl.pallas_call(
        functools.partial(_ffn_kernel, n_in=n_in),
        grid=(bsz, rows // tm),
        in_specs=h_specs + [full(wg), full(wu), full(wd), full(g), full(b)],
        out_specs=pl.BlockSpec((None, tm, d), lambda bi, i: (bi, i, 0)),
        out_shape=jax.ShapeDtypeStruct((bsz, rows, d), F32),
        compiler_params=pltpu.CompilerParams(vmem_limit_bytes=60 * 1024 * 1024),
        name="ffn",
    )(*([h] * n_in), wg, wu, wd, g, b)


def _rope_tables(lp):
    pos = jnp.maximum(jnp.arange(lp, dtype=jnp.int32) - PAD, 0).astype(F32)
    inv = ROPE_THETA ** (-jnp.arange(0, MLA_ROPE, 2, dtype=F32) / MLA_ROPE)
    ang = pos[:, None] * inv[None, :]
    cos, sin = jnp.cos(ang), jnp.sin(ang)
    z16 = jnp.zeros((lp, ROPE_HALF), F32)
    z32 = jnp.zeros((lp, LANES - MLA_NOPE - MLA_ROPE), F32)
    one = jnp.ones((lp, MLA_NOPE), F32)
    z64 = jnp.zeros((lp, MLA_NOPE), F32)
    c = jnp.concatenate([one, cos, cos, z32], -1)
    sa = jnp.concatenate([z64, -sin, z16, z32], -1)
    sb = jnp.concatenate([z64, z16, sin, z32], -1)
    return c, sa, sb


def _layout_w_in(w):
    d = w.shape[0]
    o = 0
    q_c = w[:, o:o + MLA_Q_RANK]; o += MLA_Q_RANK
    kv_c = w[:, o:o + MLA_KV_RANK]; o += MLA_KV_RANK
    k_r = w[:, o:o + MLA_ROPE]; o += MLA_ROPE
    rest = w[:, o:]
    k_r_p = jnp.concatenate([jnp.zeros((d, MLA_NOPE), w.dtype), k_r,
                             jnp.zeros((d, LANES - MLA_NOPE - MLA_ROPE), w.dtype)], -1)
    return jnp.concatenate([q_c, kv_c, k_r_p, rest], -1).astype(BF16)


def _layout_w_uq(w):
    w = w.reshape(MLA_Q_RANK, MLA_HEADS, MLA_NOPE + MLA_ROPE)
    w = jnp.pad(w, ((0, 0), (0, 0), (0, HEAD_PAD - MLA_NOPE - MLA_ROPE)))
    return w.reshape(MLA_Q_RANK, MLA_HEADS * HEAD_PAD).astype(BF16)


def _layout_w_ukv(w):
    w = w.reshape(MLA_KV_RANK, MLA_HEADS, MLA_NOPE + MLA_V)
    k = jnp.pad(w[..., :MLA_NOPE], ((0, 0), (0, 0), (0, HEAD_PAD - MLA_NOPE)))
    v = w[..., MLA_NOPE:]
    return jnp.concatenate([k.reshape(MLA_KV_RANK, -1), v.reshape(MLA_KV_RANK, -1)], -1).astype(BF16)


def kernel(x, meta_tokens, ln_in_g, ln_in_b, w_in, q_norm_g, w_uq, kv_norm_g, w_ukv, attn_sinks,
           grp_norm_a, grp_norm_b, w_out, ln1_g, ln1_b, w_gate_up, w_down, ln2_g, ln2_b):
    bsz, s, d = x.shape
    assert d == D_MODEL and s % ATT_TILE == 0
    lp = s + BLOCK
    tm = _row_tile(lp)
    vec = lambda a: a.reshape(1, -1)

    meta_pad = jnp.pad(meta_tokens.astype(x.dtype), ((PAD, 0), (0, 0)))
    h = _ln_in(x, meta_pad, vec(ln_in_g), vec(ln_in_b), tm)
    c, sa, sb = _rope_tables(lp)

    for l in range(DEPTH):
        q, k, v, sq, skd, svd = _proj(
            h, _layout_w_in(w_in[l]), vec(q_norm_g[l]), _layout_w_uq(w_uq[l]),
            vec(kv_norm_g[l]), _layout_w_ukv(w_ukv[l]), c, sa, sb, tm)
        o_a = _mla(q, k, v)
        o_b = _swa(attn_sinks[l], sq, skd, svd, tm)
        h = _mix(o_a, o_b, h, vec(grp_norm_a[l]), vec(grp_norm_b[l]), w_out[l].astype(BF16),
                 vec(ln1_g[l]), vec(ln1_b[l]), tm)
        final = l == DEPTH - 1
        h = _ffn(h, w_gate_up[l, :, :D_FF].astype(BF16), w_gate_up[l, :, D_FF:].astype(BF16),
                 w_down[l].astype(BF16), vec(ln2_g[l]), vec(ln2_b[l]), ATT_TILE if final else tm, final)
    return h
```

```python
import functools

import jax
import jax.numpy as jnp
import numpy as np
from jax import lax
from jax.experimental import pallas as pl
from jax.experimental.pallas import tpu as pltpu

D_MODEL = 1024
DEPTH = 4
N_META = 16
BLOCK = 128
PAD = BLOCK - N_META
MLA_HEADS = 8
MLA_Q_RANK = 256
MLA_KV_RANK = 128
MLA_NOPE = 64
MLA_ROPE = 32
MLA_V = 64
ROPE_THETA = 10000.0
SWA_HEADS = 8
SWA_KV_HEADS = 2
SWA_GROUP = SWA_HEADS // SWA_KV_HEADS
SWA_HEAD_DIM = 64
WINDOW = 128
MIX_A = MLA_HEADS * MLA_V
MIX_B = SWA_HEADS * SWA_HEAD_DIM
D_FF = 2816
ALPHA = (2 * DEPTH) ** 0.25
LN_EPS = 1e-5
RMS_EPS = 1e-6
NEG = -1e30

LANES = 128
HEAD_PAD = LANES
ROPE_HALF = MLA_ROPE // 2
ATT_TILE = 512
LOG2E = float(np.log2(np.e))
SWA_BIAS_VARIANTS = 3

BF16 = jnp.bfloat16
F32 = jnp.float32


def _row_tile(lp):
    for t in (512, 384, 256, 128):
        if lp % t == 0:
            return t
    raise ValueError(f"padded length {lp} is not a multiple of {BLOCK}")


def _layer_norm(z, g, b):
    mu = jnp.mean(z, -1, keepdims=True)
    zc = z - mu
    var = jnp.mean(zc * zc, -1, keepdims=True)
    return zc * lax.rsqrt(var + LN_EPS) * g + b


def _rms_norm(z, g):
    return z * lax.rsqrt(jnp.mean(z * z, -1, keepdims=True) + RMS_EPS) * g


def _dot(a, b):
    return jnp.dot(a, b, preferred_element_type=F32)


def _dot_nt(a, b):
    return lax.dot_general(a, b, (((1,), (1,)), ((), ())), preferred_element_type=F32)


def _ln_in_kernel(*refs, n_sub):
    x_refs, (meta_ref, g_ref, b_ref, o_ref) = refs[:n_sub], refs[n_sub:]
    i = pl.program_id(1)
    for t, x_ref in enumerate(x_refs):
        z = x_ref[...]
        if t == 0:
            z = jnp.where(i == 0, meta_ref[...], z)
        o_ref[t * BLOCK:(t + 1) * BLOCK, :] = _layer_norm(z, g_ref[...], b_ref[...])


def _ln_in(x, meta_pad, g, b, tm):
    bsz, s, d = x.shape
    lp = s + BLOCK
    n_sub = tm // BLOCK
    x_spec = lambda t: pl.BlockSpec((None, BLOCK, d), lambda bi, i: (bi, jnp.maximum(i * n_sub + t - 1, 0), 0))
    full = lambda a: pl.BlockSpec(a.shape, lambda bi, i: (0,) * a.ndim)
    return pl.pallas_call(
        functools.partial(_ln_in_kernel, n_sub=n_sub),
        grid=(bsz, lp // tm),
        in_specs=[x_spec(t) for t in range(n_sub)] + [full(meta_pad), full(g), full(b)],
        out_specs=pl.BlockSpec((None, tm, d), lambda bi, i: (bi, i, 0)),
        out_shape=jax.ShapeDtypeStruct((bsz, lp, d), F32),
        name="ln_in",
    )(*([x] * n_sub), meta_pad, g, b)


def _rope(z, c, sa, sb):
    return z * c + pltpu.roll(z, LANES - ROPE_HALF, 1) * sa + pltpu.roll(z, ROPE_HALF, 1) * sb


def _proj_kernel(h_ref, w_in_ref, gq_ref, w_uq_ref, gkv_ref, w_ukv_ref, c_ref, sa_ref, sb_ref,
                 q_ref, k_ref, v_ref, sq_ref, sk_ref, sv_ref):
    x = h_ref[...].astype(BF16)
    proj = _dot(x, w_in_ref[...])
    c, sa, sb = c_ref[...], sa_ref[...], sb_ref[...]
    o = 0
    q_c = proj[:, o:o + MLA_Q_RANK]; o += MLA_Q_RANK
    kv_c = proj[:, o:o + MLA_KV_RANK]; o += MLA_KV_RANK
    k_r = proj[:, o:o + LANES]; o += LANES
    s_q = proj[:, o:o + MIX_B]; o += MIX_B
    s_k = proj[:, o:o + LANES]; o += LANES
    s_v = proj[:, o:o + LANES]

    q = _dot(_rms_norm(q_c, gq_ref[...]).astype(BF16), w_uq_ref[...])
    q_scale = (MLA_NOPE + MLA_ROPE) ** -0.5 * LOG2E
    for h in range(MLA_HEADS):
        sl = slice(h * HEAD_PAD, (h + 1) * HEAD_PAD)
        q_ref[:, sl] = (_rope(q[:, sl], c, sa, sb) * q_scale).astype(BF16)

    kv = _dot(_rms_norm(kv_c, gkv_ref[...]).astype(BF16), w_ukv_ref[...])
    k_rope = _rope(k_r, c, sa, sb)
    for h in range(MLA_HEADS):
        sl = slice(h * HEAD_PAD, (h + 1) * HEAD_PAD)
        k_ref[:, sl] = (kv[:, sl] + k_rope).astype(BF16)
    v_ref[...] = kv[:, MLA_HEADS * HEAD_PAD:].astype(BF16)

    sq_ref[...] = (s_q * (SWA_HEAD_DIM ** -0.5 * LOG2E)).astype(BF16)
    low = lax.broadcasted_iota(jnp.int32, s_k.shape, 1) < SWA_HEAD_DIM
    for src, dst in ((s_k, sk_ref), (s_v, sv_ref)):
        swapped = pltpu.roll(src, SWA_HEAD_DIM, 1)
        dst[:, :LANES] = jnp.where(low, src, swapped).astype(BF16)
        dst[:, LANES:] = jnp.where(low, swapped, src).astype(BF16)


def _proj(h, w_in_p, gq, w_uq_p, gkv, w_ukv_p, c, sa, sb, tm):
    bsz, lp, d = h.shape
    nt = lp // tm
    row = lambda w: pl.BlockSpec((None, tm, w), lambda bi, i: (bi, i, 0))
    full = lambda a: pl.BlockSpec(a.shape, lambda bi, i: (0,) * a.ndim)
    tab = pl.BlockSpec((tm, LANES), lambda bi, i: (i, 0))
    widths = (MLA_HEADS * HEAD_PAD, MLA_HEADS * HEAD_PAD, MIX_A, MIX_B, 2 * LANES, 2 * LANES)
    return pl.pallas_call(
        _proj_kernel,
        grid=(bsz, nt),
        in_specs=[row(d), full(w_in_p), full(gq), full(w_uq_p), full(gkv), full(w_ukv_p), tab, tab, tab],
        out_specs=[row(w) for w in widths],
        out_shape=[jax.ShapeDtypeStruct((bsz, lp, w), BF16) for w in widths],
        name="proj",
    )(h, w_in_p, gq, w_uq_p, gkv, w_ukv_p, c, sa, sb)


def _mla_schedule(n_tiles):
    rows = []
    for j in range(n_tiles):
        for c in range(j + 1):
            rows.append((BLOCK + j * ATT_TILE, BLOCK + c * ATT_TILE, j, int(c == 0), int(c == j)))
    return np.asarray(rows, np.int32)


def _mla_kernel(sched_ref, q_ref, k_ref, v_ref, o_ref,
                s_ref, p_ref, alpha_ref, m_ref, acc_ref, meta_m_ref, meta_acc_ref, *, n_tiles):
    n_steps = n_tiles * (n_tiles + 1) // 2
    lane = lax.broadcasted_iota(jnp.int32, (1, 2 * LANES), 1)
    head_a = (lane % LANES) < MLA_V
    tr = lax.broadcasted_iota(jnp.int32, (ATT_TILE, ATT_TILE), 0)
    tc = lax.broadcasted_iota(jnp.int32, (ATT_TILE, ATT_TILE), 1)
    diag_mask = tc <= tr

    def sched(t, col):
        return sched_ref[t, col]

    def rows_at(off):
        return pl.ds(pl.multiple_of(off, BLOCK), ATT_TILE)

    def value_matrix(vc):
        n = vc.shape[0]
        lane_i = lax.broadcasted_iota(jnp.int32, (n, LANES), 1)
        low = lane_i < MLA_V
        zero = jnp.zeros((n, LANES), BF16)
        one_lo = jnp.clip(MLA_V - lane_i, 0, 1).astype(F32)
        top = jnp.concatenate([jnp.where(low, vc, zero), one_lo.astype(BF16)], axis=1)
        bot = jnp.concatenate([jnp.where(low, zero, vc), (1.0 - one_lo).astype(BF16)], axis=1)
        return jnp.concatenate([top, bot], axis=0)

    def heads(x):
        return x[:, :HEAD_PAD], x[:, HEAD_PAD:]

    def normalized(acc):
        return (acc[:, :LANES] / acc[:, LANES:]).astype(BF16)

    def meta_attention(q, mask):
        ms, ps = [], []
        for qh, kh in zip(heads(q), heads(k_ref[:BLOCK, :])):
            s = jnp.where(mask, _dot_nt(qh, kh), NEG)
            m = jnp.max(s, -1, keepdims=True)
            ms.append(m)
            ps.append(jnp.exp2(s - m).astype(BF16))
        return ms, _dot(jnp.concatenate(ps, axis=1), value_matrix(v_ref[:BLOCK, :]))

    def score_stage(t, slot):
        q, kc = q_ref[rows_at(sched(t, 0)), :], k_ref[rows_at(sched(t, 1)), :]
        for h, (qh, kh) in enumerate(zip(heads(q), heads(kc))):
            s_ref[slot, h] = _dot_nt(qh, kh)

    def mask_stage(t, slot):
        @pl.when(sched(t, 4) == 1)
        def _():
            for h in range(2):
                s_ref[slot, h] = jnp.where(diag_mask, s_ref[slot, h], NEG)

    def softmax_stage(t, slot):
        first = sched(t, 3) == 1
        alphas = []
        for h in range(2):
            s = s_ref[slot, h]
            m_prev = jnp.where(first, meta_m_ref[sched(t, 2), h], m_ref[h])
            m_new = jnp.maximum(m_prev, jnp.max(s, -1, keepdims=True))
            alphas.append(jnp.exp2(m_prev - m_new))
            m_ref[h] = m_new
            p_ref[slot, :, h * ATT_TILE:(h + 1) * ATT_TILE] = jnp.exp2(s - m_new).astype(BF16)
        alpha_ref[slot] = jnp.where(head_a, alphas[0], alphas[1])

    def value_stage(t, slot):
        first = sched(t, 3) == 1
        pv = _dot(p_ref[slot], value_matrix(v_ref[rows_at(sched(t, 1)), :]))
        base = jnp.where(first, meta_acc_ref[sched(t, 2)], acc_ref[...])
        acc_ref[...] = alpha_ref[slot] * base + pv

    def output_stage(t):
        @pl.when(sched(t, 4) == 1)
        def _():
            o_ref[rows_at(sched(t, 0)), :] = normalized(acc_ref[...])

    def step(t, slot):
        score_stage(t + 1, 1 - slot)
        value_stage(t - 1, 1 - slot)
        softmax_stage(t, slot)
        output_stage(t - 1)
        mask_stage(t + 1, 1 - slot)

    r = lax.broadcasted_iota(jnp.int32, (BLOCK, BLOCK), 0)
    cidx = lax.broadcasted_iota(jnp.int32, (BLOCK, BLOCK), 1)
    _, acc0 = meta_attention(q_ref[:BLOCK, :], (cidx <= r) & (cidx >= PAD))
    o_ref[:BLOCK, :] = normalized(acc0)

    meta_mask = lax.broadcasted_iota(jnp.int32, (ATT_TILE, BLOCK), 1) >= PAD
    for j in range(n_tiles):
        ms, acc = meta_attention(q_ref[BLOCK + j * ATT_TILE:BLOCK + (j + 1) * ATT_TILE, :], meta_mask)
        meta_acc_ref[j] = acc
        for h in range(2):
            meta_m_ref[j, h] = ms[h]

    score_stage(0, 0)
    mask_stage(0, 0)
    if n_steps == 1:
        softmax_stage(0, 0)
    else:
        score_stage(1, 1)
        softmax_stage(0, 0)
        mask_stage(1, 1)
        n_mid = n_steps - 2

        def two_steps(i, carry):
            t = 1 + 2 * i
            step(t, 1)
            step(t + 1, 0)
            return carry

        lax.fori_loop(0, n_mid // 2, two_steps, 0)
        if n_mid % 2:
            step(n_steps - 2, (n_steps - 2) % 2)
        last = n_steps - 1
        value_stage(last - 1, (last - 1) % 2)
        softmax_stage(last, last % 2)
        output_stage(last - 1)
    value_stage(n_steps - 1, (n_steps - 1) % 2)
    output_stage(n_steps - 1)


def _mla(q, k, v):
    bsz, lp, _ = q.shape
    n_pairs = MLA_HEADS // 2
    n_tiles = (lp - BLOCK) // ATT_TILE
    sched = jnp.asarray(_mla_schedule(n_tiles))
    qk_spec = pl.BlockSpec((None, lp, 2 * HEAD_PAD), lambda bi, p: (bi, 0, p))
    v_spec = pl.BlockSpec((None, lp, 2 * MLA_V), lambda bi, p: (bi, 0, p))
    return pl.pallas_call(
        functools.partial(_mla_kernel, n_tiles=n_tiles),
        grid=(bsz, n_pairs),
        in_specs=[pl.BlockSpec(memory_space=pltpu.SMEM), qk_spec, qk_spec, v_spec],
        out_specs=v_spec,
        out_shape=jax.ShapeDtypeStruct((bsz, lp, MIX_A), BF16),
        scratch_shapes=[
            pltpu.VMEM((2, 2, ATT_TILE, ATT_TILE), F32),
            pltpu.VMEM((2, ATT_TILE, 2 * ATT_TILE), BF16),
            pltpu.VMEM((2, ATT_TILE, 2 * LANES), F32),
            pltpu.VMEM((2, ATT_TILE, 1), F32),
            pltpu.VMEM((ATT_TILE, 2 * LANES), F32),
            pltpu.VMEM((n_tiles, 2, ATT_TILE, 1), F32),
            pltpu.VMEM((n_tiles, ATT_TILE, 2 * LANES), F32),
        ],
        name="mla",
    )(sched, q, k, v)


def _swa_kernel(sinks_ref, bias_ref, q_ref, kp_ref, kc_ref, vp_ref, vc_ref, o_ref, *, n_sub):
    first_block = pl.program_id(1) * n_sub
    low = lax.broadcasted_iota(jnp.int32, (BLOCK, LANES), 1) < SWA_HEAD_DIM
    zero = jnp.zeros((BLOCK, LANES), BF16)

    for t in range(n_sub):
        rows = slice(t * BLOCK, (t + 1) * BLOCK)
        variant = jnp.minimum(first_block + t, SWA_BIAS_VARIANTS - 1)
        for kvh in range(SWA_KV_HEADS):
            sl = slice(kvh * LANES, (kvh + 1) * LANES)
            if t == 0:
                k_band = jnp.concatenate([kp_ref[:, sl], kc_ref[rows, sl]], axis=0)
                v_band = jnp.concatenate([vp_ref[:, sl], vc_ref[rows, sl]], axis=0)
            else:
                band = slice((t - 1) * BLOCK, (t + 1) * BLOCK)
                k_band, v_band = kc_ref[band, sl], vc_ref[band, sl]
            q_rows = []
            for pair in range(SWA_GROUP // 2):
                g = kvh * (SWA_GROUP // 2) + pair
                qp = q_ref[rows, g * LANES:(g + 1) * LANES]
                q_rows += [jnp.where(low, qp, zero), jnp.where(low, zero, qp)]
            s_all = _dot_nt(jnp.concatenate(q_rows, axis=0), k_band)
            probs = []
            for gi in range(SWA_GROUP):
                head = kvh * SWA_GROUP + gi
                sink = sinks_ref[head] * LOG2E
                s = s_all[gi * BLOCK:(gi + 1) * BLOCK] + bias_ref[variant, head]
                m = jnp.maximum(jnp.max(s, -1, keepdims=True), sink)
                e = jnp.exp2(s - m)
                p = e / (jnp.sum(e, -1, keepdims=True) + jnp.exp2(sink - m))
                probs.append(p.astype(BF16))
            o_all = _dot(jnp.concatenate(probs, axis=0), v_band)
            for pair in range(SWA_GROUP // 2):
                g = kvh * (SWA_GROUP // 2) + pair
                o_lo = o_all[(2 * pair) * BLOCK:(2 * pair + 1) * BLOCK]
                o_hi = o_all[(2 * pair + 1) * BLOCK:(2 * pair + 2) * BLOCK]
                o_ref[rows, g * LANES:(g + 1) * LANES] = jnp.where(low, o_lo, o_hi).astype(BF16)


def _swa_bias():
    row = jnp.arange(BLOCK)[:, None]
    col = jnp.arange(2 * BLOCK)[None, :]
    dist = BLOCK + row - col
    in_band = (dist >= 0) & (dist < WINDOW)
    slopes = jnp.exp2(-8.0 * jnp.arange(1, SWA_HEADS + 1, dtype=F32) / SWA_HEADS)
    alibi = -slopes[:, None, None] * dist.astype(F32)[None] * LOG2E
    variants = []
    for block in range(SWA_BIAS_VARIANTS):
        valid = in_band & ((block - 1) * BLOCK + col >= PAD)
        variants.append(jnp.where(valid[None], alibi, NEG))
    return jnp.stack(variants)


def _swa(sinks, bias, sq, skd, svd, tm):
    bsz, lp, _ = sq.shape
    n_sub = tm // BLOCK
    cur = lambda w: pl.BlockSpec((None, tm, w), lambda bi, i: (bi, i, 0))
    prev = lambda w: pl.BlockSpec((None, BLOCK, w), lambda bi, i: (bi, jnp.maximum(i * n_sub - 1, 0), 0))
    return pl.pallas_call(
        functools.partial(_swa_kernel, n_sub=n_sub),
        grid=(bsz, lp // tm),
        in_specs=[pl.BlockSpec(memory_space=pltpu.SMEM),
                  pl.BlockSpec(bias.shape, lambda bi, i: (0, 0, 0, 0), pipeline_mode=pl.Buffered(1)),
                  cur(MIX_B), prev(2 * LANES), cur(2 * LANES), prev(2 * LANES), cur(2 * LANES)],
        out_specs=cur(MIX_B),
        out_shape=jax.ShapeDtypeStruct((bsz, lp, MIX_B), BF16),
        name="swa",
    )(sinks, bias, sq, skd, skd, svd, svd)


def _mix_ffn_kernel(*refs, n_in):
    rows = lambda k: refs[k * n_in:(k + 1) * n_in]
    (ga_ref, gb_ref, wo_ref, g1_ref, b1_ref, wg_ref, wu_ref, wd_ref, g2_ref, b2_ref, o_ref) = refs[3 * n_in:]
    tile = lambda rs: jnp.concatenate([r[...] for r in rs], axis=0) if n_in > 1 else rs[0][...]
    a = _rms_norm(tile(rows(0)).astype(F32), ga_ref[...]).astype(BF16)
    b = _rms_norm(tile(rows(1)).astype(F32), gb_ref[...]).astype(BF16)
    y = _dot(jnp.concatenate([a, b], axis=-1), wo_ref[...])
    x = _layer_norm(ALPHA * tile(rows(2)) + y, g1_ref[...], b1_ref[...])
    xb = x.astype(BF16)
    gate = _dot(xb, wg_ref[...])
    up = _dot(xb, wu_ref[...])
    act = (gate * (1.0 / (1.0 + jnp.exp(-gate))) * up).astype(BF16)
    f = _dot(act, wd_ref[...])
    o_ref[...] = _layer_norm(ALPHA * x + f, g2_ref[...], b2_ref[...])


def _mix_ffn(o_a, o_b, h, params, tm, final):
    bsz, lp, d = h.shape
    full = lambda a: pl.BlockSpec(a.shape, lambda bi, i: (0,) * a.ndim, pipeline_mode=pl.Buffered(1))
    if final:
        rows, n_in = lp - BLOCK, tm // BLOCK
        row_specs = lambda w: [pl.BlockSpec((None, BLOCK, w), lambda bi, i, t=t: (bi, 1 + i * n_in + t, 0))
                               for t in range(n_in)]
    else:
        rows, n_in = lp, 1
        row_specs = lambda w: [pl.BlockSpec((None, tm, w), lambda bi, i: (bi, i, 0))]
    return pl.pallas_call(
        functools.partial(_mix_ffn_kernel, n_in=n_in),
        grid=(bsz, rows // tm),
        in_specs=row_specs(MIX_A) + row_specs(MIX_B) + row_specs(d) + [full(p) for p in params],
        out_specs=pl.BlockSpec((None, tm, d), lambda bi, i: (bi, i, 0)),
        out_shape=jax.ShapeDtypeStruct((bsz, rows, d), F32),
        compiler_params=pltpu.CompilerParams(vmem_limit_bytes=60 * 1024 * 1024),
        name="mix_ffn",
    )(*([o_a] * n_in), *([o_b] * n_in), *([h] * n_in), *params)


def _rope_tables(lp):
    pos = jnp.maximum(jnp.arange(lp, dtype=jnp.int32) - PAD, 0).astype(F32)
    inv = ROPE_THETA ** (-jnp.arange(0, MLA_ROPE, 2, dtype=F32) / MLA_ROPE)
    ang = pos[:, None] * inv[None, :]
    cos, sin = jnp.cos(ang), jnp.sin(ang)
    z16 = jnp.zeros((lp, ROPE_HALF), F32)
    z32 = jnp.zeros((lp, LANES - MLA_NOPE - MLA_ROPE), F32)
    one = jnp.ones((lp, MLA_NOPE), F32)
    z64 = jnp.zeros((lp, MLA_NOPE), F32)
    c = jnp.concatenate([one, cos, cos, z32], -1)
    sa = jnp.concatenate([z64, -sin, z16, z32], -1)
    sb = jnp.concatenate([z64, z16, sin, z32], -1)
    return c, sa, sb


def _layout_w_in(w):
    d = w.shape[0]
    o = 0
    q_c = w[:, o:o + MLA_Q_RANK]; o += MLA_Q_RANK
    kv_c = w[:, o:o + MLA_KV_RANK]; o += MLA_KV_RANK
    k_r = w[:, o:o + MLA_ROPE]; o += MLA_ROPE
    rest = w[:, o:]
    k_r_p = jnp.concatenate([jnp.zeros((d, MLA_NOPE), w.dtype), k_r,
                             jnp.zeros((d, LANES - MLA_NOPE - MLA_ROPE), w.dtype)], -1)
    return jnp.concatenate([q_c, kv_c, k_r_p, rest], -1).astype(BF16)


def _layout_w_uq(w):
    w = w.reshape(MLA_Q_RANK, MLA_HEADS, MLA_NOPE + MLA_ROPE)
    w = jnp.pad(w, ((0, 0), (0, 0), (0, HEAD_PAD - MLA_NOPE - MLA_ROPE)))
    return w.reshape(MLA_Q_RANK, MLA_HEADS * HEAD_PAD).astype(BF16)


def _layout_w_ukv(w):
    w = w.reshape(MLA_KV_RANK, MLA_HEADS, MLA_NOPE + MLA_V)
    k = jnp.pad(w[..., :MLA_NOPE], ((0, 0), (0, 0), (0, HEAD_PAD - MLA_NOPE)))
    v = w[..., MLA_NOPE:]
    return jnp.concatenate([k.reshape(MLA_KV_RANK, -1), v.reshape(MLA_KV_RANK, -1)], -1).astype(BF16)


def kernel(x, meta_tokens, ln_in_g, ln_in_b, w_in, q_norm_g, w_uq, kv_norm_g, w_ukv, attn_sinks,
           grp_norm_a, grp_norm_b, w_out, ln1_g, ln1_b, w_gate_up, w_down, ln2_g, ln2_b):
    bsz, s, d = x.shape
    assert d == D_MODEL and s % ATT_TILE == 0
    lp = s + BLOCK
    tm = _row_tile(lp)
    vec = lambda a: a.reshape(1, -1)

    meta_pad = jnp.pad(meta_tokens.astype(x.dtype), ((PAD, 0), (0, 0)))
    h = _ln_in(x, meta_pad, vec(ln_in_g), vec(ln_in_b), tm)
    c, sa, sb = _rope_tables(lp)
    swa_bias = _swa_bias()

    for l in range(DEPTH):
        q, k, v, sq, skd, svd = _proj(
            h, _layout_w_in(w_in[l]), vec(q_norm_g[l]), _layout_w_uq(w_uq[l]),
            vec(kv_norm_g[l]), _layout_w_ukv(w_ukv[l]), c, sa, sb, tm)
        o_a = _mla(q, k, v)
        o_b = _swa(attn_sinks[l], swa_bias, sq, skd, svd, tm)
        params = (vec(grp_norm_a[l]), vec(grp_norm_b[l]), w_out[l].astype(BF16), vec(ln1_g[l]), vec(ln1_b[l]),
                  w_gate_up[l, :, :D_FF].astype(BF16), w_gate_up[l, :, D_FF:].astype(BF16),
                  w_down[l].astype(BF16), vec(ln2_g[l]), vec(ln2_b[l]))
        final = l == DEPTH - 1
        h = _mix_ffn(o_a, o_b, h, params, ATT_TILE if final else tm, final)
    return h
```

```python
import functools

import jax
import jax.numpy as jnp
import numpy as np
from jax import lax
from jax.experimental import pallas as pl
from jax.experimental.pallas import tpu as pltpu

D_MODEL = 1024
DEPTH = 4
N_META = 16
BLOCK = 128
PAD = BLOCK - N_META
MLA_HEADS = 8
MLA_Q_RANK = 256
MLA_KV_RANK = 128
MLA_NOPE = 64
MLA_ROPE = 32
MLA_V = 64
ROPE_THETA = 10000.0
SWA_HEADS = 8
SWA_KV_HEADS = 2
SWA_GROUP = SWA_HEADS // SWA_KV_HEADS
SWA_HEAD_DIM = 64
WINDOW = 128
MIX_A = MLA_HEADS * MLA_V
MIX_B = SWA_HEADS * SWA_HEAD_DIM
D_FF = 2816
ALPHA = (2 * DEPTH) ** 0.25
LN_EPS = 1e-5
RMS_EPS = 1e-6
NEG = -1e30

LANES = 128
HEAD_PAD = LANES
ROPE_HALF = MLA_ROPE // 2
ATT_TILE = 512
FINAL_ROW_TILE = ATT_TILE
LOG2E = float(np.log2(np.e))
SWA_BIAS_VARIANTS = 3

BF16 = jnp.bfloat16
F32 = jnp.float32


def _row_tile(lp):
    for t in (512, 384, 256, 128):
        if lp % t == 0:
            return t
    raise ValueError(f"padded length {lp} is not a multiple of {BLOCK}")


def _layer_norm(z, g, b):
    mu = jnp.mean(z, -1, keepdims=True)
    zc = z - mu
    var = jnp.mean(zc * zc, -1, keepdims=True)
    return zc * lax.rsqrt(var + LN_EPS) * g + b


def _rms_norm(z, g):
    return z * lax.rsqrt(jnp.mean(z * z, -1, keepdims=True) + RMS_EPS) * g


def _dot(a, b):
    return jnp.dot(a, b, preferred_element_type=F32)


def _dot_nt(a, b):
    return lax.dot_general(a, b, (((1,), (1,)), ((), ())), preferred_element_type=F32)


def _ln_in_kernel(*refs, n_sub):
    x_refs, (meta_ref, g_ref, b_ref, o_ref) = refs[:n_sub], refs[n_sub:]
    i = pl.program_id(1)
    for t, x_ref in enumerate(x_refs):
        z = x_ref[...]
        if t == 0:
            z = jnp.where(i == 0, meta_ref[...], z)
        o_ref[t * BLOCK:(t + 1) * BLOCK, :] = _layer_norm(z, g_ref[...], b_ref[...])


def _ln_in(x, meta_pad, g, b, tm):
    bsz, s, d = x.shape
    lp = s + BLOCK
    n_sub = tm // BLOCK
    x_spec = lambda t: pl.BlockSpec((None, BLOCK, d), lambda bi, i: (bi, jnp.maximum(i * n_sub + t - 1, 0), 0))
    full = lambda a: pl.BlockSpec(a.shape, lambda bi, i: (0,) * a.ndim)
    return pl.pallas_call(
        functools.partial(_ln_in_kernel, n_sub=n_sub),
        grid=(bsz, lp // tm),
        in_specs=[x_spec(t) for t in range(n_sub)] + [full(meta_pad), full(g), full(b)],
        out_specs=pl.BlockSpec((None, tm, d), lambda bi, i: (bi, i, 0)),
        out_shape=jax.ShapeDtypeStruct((bsz, lp, d), F32),
        name="ln_in",
    )(*([x] * n_sub), meta_pad, g, b)


def _rope(z, c, sa, sb):
    return z * c + pltpu.roll(z, LANES - ROPE_HALF, 1) * sa + pltpu.roll(z, ROPE_HALF, 1) * sb


def _proj_kernel(h_ref, w_in_ref, gq_ref, w_uq_ref, gkv_ref, w_ukv_ref, c_ref, sa_ref, sb_ref,
                 q_ref, k_ref, v_ref, sq_ref, sk_ref, sv_ref):
    x = h_ref[...].astype(BF16)
    proj = _dot(x, w_in_ref[...])
    c, sa, sb = c_ref[...], sa_ref[...], sb_ref[...]
    o = 0
    q_c = proj[:, o:o + MLA_Q_RANK]; o += MLA_Q_RANK
    kv_c = proj[:, o:o + MLA_KV_RANK]; o += MLA_KV_RANK
    k_r = proj[:, o:o + LANES]; o += LANES
    s_q = proj[:, o:o + MIX_B]; o += MIX_B
    s_k = proj[:, o:o + LANES]; o += LANES
    s_v = proj[:, o:o + LANES]

    q = _dot(_rms_norm(q_c, gq_ref[...]).astype(BF16), w_uq_ref[...])
    q_scale = (MLA_NOPE + MLA_ROPE) ** -0.5 * LOG2E
    for h in range(MLA_HEADS):
        sl = slice(h * HEAD_PAD, (h + 1) * HEAD_PAD)
        q_ref[:, sl] = (_rope(q[:, sl], c, sa, sb) * q_scale).astype(BF16)

    kv = _dot(_rms_norm(kv_c, gkv_ref[...]).astype(BF16), w_ukv_ref[...])
    k_rope = _rope(k_r, c, sa, sb)
    for h in range(MLA_HEADS):
        sl = slice(h * HEAD_PAD, (h + 1) * HEAD_PAD)
        k_ref[:, sl] = (kv[:, sl] + k_rope).astype(BF16)
    v_ref[...] = kv[:, MLA_HEADS * HEAD_PAD:].astype(BF16)

    sq_ref[...] = (s_q * (SWA_HEAD_DIM ** -0.5 * LOG2E)).astype(BF16)
    low = lax.broadcasted_iota(jnp.int32, s_k.shape, 1) < SWA_HEAD_DIM
    for src, dst in ((s_k, sk_ref), (s_v, sv_ref)):
        swapped = pltpu.roll(src, SWA_HEAD_DIM, 1)
        dst[:, :LANES] = jnp.where(low, src, swapped).astype(BF16)
        dst[:, LANES:] = jnp.where(low, swapped, src).astype(BF16)


def _layer_spec(a, layer, col_block=0, n_col_blocks=1):
    _, rows, cols = a.shape
    return pl.BlockSpec((None, rows, cols // n_col_blocks), lambda bi, i: (layer, 0, col_block),
                        pipeline_mode=pl.Buffered(1))


def _proj(h, layer, w_in_p, gq, w_uq_p, gkv, w_ukv_p, c, sa, sb, tm):
    bsz, lp, d = h.shape
    nt = lp // tm
    row = lambda w: pl.BlockSpec((None, tm, w), lambda bi, i: (bi, i, 0))
    tab = pl.BlockSpec((tm, LANES), lambda bi, i: (i, 0))
    params = (w_in_p, gq, w_uq_p, gkv, w_ukv_p)
    widths = (MLA_HEADS * HEAD_PAD, MLA_HEADS * HEAD_PAD, MIX_A, MIX_B, 2 * LANES, 2 * LANES)
    return pl.pallas_call(
        _proj_kernel,
        grid=(bsz, nt),
        in_specs=[row(d)] + [_layer_spec(p, layer) for p in params] + [tab, tab, tab],
        out_specs=[row(w) for w in widths],
        out_shape=[jax.ShapeDtypeStruct((bsz, lp, w), BF16) for w in widths],
        name="proj",
    )(h, *params, c, sa, sb)


def _mla_schedule(n_tiles):
    rows = []
    for j in range(n_tiles):
        for c in range(j + 1):
            rows.append((BLOCK + j * ATT_TILE, BLOCK + c * ATT_TILE, j, int(c == 0), int(c == j)))
    return np.asarray(rows, np.int32)


def _mla_kernel(sched_ref, q_ref, k_ref, v_ref, o_ref,
                s_ref, p_ref, alpha_ref, m_ref, acc_ref, meta_m_ref, meta_acc_ref, *, n_tiles):
    n_steps = n_tiles * (n_tiles + 1) // 2
    lane = lax.broadcasted_iota(jnp.int32, (1, 2 * LANES), 1)
    head_a = (lane % LANES) < MLA_V
    tr = lax.broadcasted_iota(jnp.int32, (ATT_TILE, ATT_TILE), 0)
    tc = lax.broadcasted_iota(jnp.int32, (ATT_TILE, ATT_TILE), 1)
    diag_mask = tc <= tr

    def sched(t, col):
        return sched_ref[t, col]

    def rows_at(off):
        return pl.ds(pl.multiple_of(off, BLOCK), ATT_TILE)

    def value_matrix(vc):
        n = vc.shape[0]
        lane_i = lax.broadcasted_iota(jnp.int32, (n, LANES), 1)
        low = lane_i < MLA_V
        zero = jnp.zeros((n, LANES), BF16)
        one_lo = jnp.clip(MLA_V - lane_i, 0, 1).astype(F32)
        top = jnp.concatenate([jnp.where(low, vc, zero), one_lo.astype(BF16)], axis=1)
        bot = jnp.concatenate([jnp.where(low, zero, vc), (1.0 - one_lo).astype(BF16)], axis=1)
        return jnp.concatenate([top, bot], axis=0)

    def heads(x):
        return x[:, :HEAD_PAD], x[:, HEAD_PAD:]

    def normalized(acc):
        return (acc[:, :LANES] / acc[:, LANES:]).astype(BF16)

    def meta_attention(q, mask):
        ms, ps = [], []
        for qh, kh in zip(heads(q), heads(k_ref[:BLOCK, :])):
            s = jnp.where(mask, _dot_nt(qh, kh), NEG)
            m = jnp.max(s, -1, keepdims=True)
            ms.append(m)
            ps.append(jnp.exp2(s - m).astype(BF16))
        return ms, _dot(jnp.concatenate(ps, axis=1), value_matrix(v_ref[:BLOCK, :]))

    def score_stage(t, slot):
        q, kc = q_ref[rows_at(sched(t, 0)), :], k_ref[rows_at(sched(t, 1)), :]
        for h, (qh, kh) in enumerate(zip(heads(q), heads(kc))):
            s_ref[slot, h] = _dot_nt(qh, kh)

    def mask_stage(t, slot):
        @pl.when(sched(t, 4) == 1)
        def _():
            for h in range(2):
                s_ref[slot, h] = jnp.where(diag_mask, s_ref[slot, h], NEG)

    def softmax_stage(t, slot):
        first = sched(t, 3) == 1
        alphas = []
        for h in range(2):
            s = s_ref[slot, h]
            m_prev = jnp.where(first, meta_m_ref[sched(t, 2), h], m_ref[h])
            m_new = jnp.maximum(m_prev, jnp.max(s, -1, keepdims=True))
            alphas.append(jnp.exp2(m_prev - m_new))
            m_ref[h] = m_new
            p_ref[slot, :, h * ATT_TILE:(h + 1) * ATT_TILE] = jnp.exp2(s - m_new).astype(BF16)
        alpha_ref[slot] = jnp.where(head_a, alphas[0], alphas[1])

    def value_stage(t, slot):
        first = sched(t, 3) == 1
        pv = _dot(p_ref[slot], value_matrix(v_ref[rows_at(sched(t, 1)), :]))
        base = jnp.where(first, meta_acc_ref[sched(t, 2)], acc_ref[...])
        acc_ref[...] = alpha_ref[slot] * base + pv

    def output_stage(t):
        @pl.when(sched(t, 4) == 1)
        def _():
            o_ref[rows_at(sched(t, 0)), :] = normalized(acc_ref[...])

    def step(t, slot):
        score_stage(t + 1, 1 - slot)
        value_stage(t - 1, 1 - slot)
        softmax_stage(t, slot)
        output_stage(t - 1)
        mask_stage(t + 1, 1 - slot)

    r = lax.broadcasted_iota(jnp.int32, (BLOCK, BLOCK), 0)
    cidx = lax.broadcasted_iota(jnp.int32, (BLOCK, BLOCK), 1)
    _, acc0 = meta_attention(q_ref[:BLOCK, :], (cidx <= r) & (cidx >= PAD))
    o_ref[:BLOCK, :] = normalized(acc0)

    meta_mask = lax.broadcasted_iota(jnp.int32, (ATT_TILE, BLOCK), 1) >= PAD
    for j in range(n_tiles):
        ms, acc = meta_attention(q_ref[BLOCK + j * ATT_TILE:BLOCK + (j + 1) * ATT_TILE, :], meta_mask)
        meta_acc_ref[j] = acc
        for h in range(2):
            meta_m_ref[j, h] = ms[h]

    score_stage(0, 0)
    mask_stage(0, 0)
    if n_steps == 1:
        softmax_stage(0, 0)
    else:
        score_stage(1, 1)
        softmax_stage(0, 0)
        mask_stage(1, 1)
        n_mid = n_steps - 2

        def two_steps(i, carry):
            t = 1 + 2 * i
            step(t, 1)
            step(t + 1, 0)
            return carry

        lax.fori_loop(0, n_mid // 2, two_steps, 0)
        if n_mid % 2:
            step(n_steps - 2, (n_steps - 2) % 2)
        last = n_steps - 1
        value_stage(last - 1, (last - 1) % 2)
        softmax_stage(last, last % 2)
        output_stage(last - 1)
    value_stage(n_steps - 1, (n_steps - 1) % 2)
    output_stage(n_steps - 1)


def _mla(q, k, v):
    bsz, lp, _ = q.shape
    n_pairs = MLA_HEADS // 2
    n_tiles = (lp - BLOCK) // ATT_TILE
    sched = jnp.asarray(_mla_schedule(n_tiles))
    qk_spec = pl.BlockSpec((None, lp, 2 * HEAD_PAD), lambda bi, p: (bi, 0, p))
    v_spec = pl.BlockSpec((None, lp, 2 * MLA_V), lambda bi, p: (bi, 0, p))
    return pl.pallas_call(
        functools.partial(_mla_kernel, n_tiles=n_tiles),
        grid=(bsz, n_pairs),
        in_specs=[pl.BlockSpec(memory_space=pltpu.SMEM), qk_spec, qk_spec, v_spec],
        out_specs=v_spec,
        out_shape=jax.ShapeDtypeStruct((bsz, lp, MIX_A), BF16),
        scratch_shapes=[
            pltpu.VMEM((2, 2, ATT_TILE, ATT_TILE), F32),
            pltpu.VMEM((2, ATT_TILE, 2 * ATT_TILE), BF16),
            pltpu.VMEM((2, ATT_TILE, 2 * LANES), F32),
            pltpu.VMEM((2, ATT_TILE, 1), F32),
            pltpu.VMEM((ATT_TILE, 2 * LANES), F32),
            pltpu.VMEM((n_tiles, 2, ATT_TILE, 1), F32),
            pltpu.VMEM((n_tiles, ATT_TILE, 2 * LANES), F32),
        ],
        name="mla",
    )(sched, q, k, v)


def _swa_kernel(sinks_ref, bias_ref, q_ref, kp_ref, kc_ref, vp_ref, vc_ref, o_ref, *, n_sub):
    first_block = pl.program_id(1) * n_sub
    low = lax.broadcasted_iota(jnp.int32, (BLOCK, LANES), 1) < SWA_HEAD_DIM
    zero = jnp.zeros((BLOCK, LANES), BF16)

    for t in range(n_sub):
        rows = slice(t * BLOCK, (t + 1) * BLOCK)
        variant = jnp.minimum(first_block + t, SWA_BIAS_VARIANTS - 1)
        for kvh in range(SWA_KV_HEADS):
            sl = slice(kvh * LANES, (kvh + 1) * LANES)
            if t == 0:
                k_band = jnp.concatenate([kp_ref[:, sl], kc_ref[rows, sl]], axis=0)
                v_band = jnp.concatenate([vp_ref[:, sl], vc_ref[rows, sl]], axis=0)
            else:
                band = slice((t - 1) * BLOCK, (t + 1) * BLOCK)
                k_band, v_band = kc_ref[band, sl], vc_ref[band, sl]
            q_rows = []
            for pair in range(SWA_GROUP // 2):
                g = kvh * (SWA_GROUP // 2) + pair
                qp = q_ref[rows, g * LANES:(g + 1) * LANES]
                q_rows += [jnp.where(low, qp, zero), jnp.where(low, zero, qp)]
            s_all = _dot_nt(jnp.concatenate(q_rows, axis=0), k_band)
            probs = []
            for gi in range(SWA_GROUP):
                head = kvh * SWA_GROUP + gi
                sink = sinks_ref[head] * LOG2E
                s = s_all[gi * BLOCK:(gi + 1) * BLOCK] + bias_ref[variant, head]
                m = jnp.maximum(jnp.max(s, -1, keepdims=True), sink)
                e = jnp.exp2(s - m)
                p = e / (jnp.sum(e, -1, keepdims=True) + jnp.exp2(sink - m))
                probs.append(p.astype(BF16))
            o_all = _dot(jnp.concatenate(probs, axis=0), v_band)
            for pair in range(SWA_GROUP // 2):
                g = kvh * (SWA_GROUP // 2) + pair
                o_lo = o_all[(2 * pair) * BLOCK:(2 * pair + 1) * BLOCK]
                o_hi = o_all[(2 * pair + 1) * BLOCK:(2 * pair + 2) * BLOCK]
                o_ref[rows, g * LANES:(g + 1) * LANES] = jnp.where(low, o_lo, o_hi).astype(BF16)


def _swa_bias():
    row = jnp.arange(BLOCK)[:, None]
    col = jnp.arange(2 * BLOCK)[None, :]
    dist = BLOCK + row - col
    in_band = (dist >= 0) & (dist < WINDOW)
    slopes = jnp.exp2(-8.0 * jnp.arange(1, SWA_HEADS + 1, dtype=F32) / SWA_HEADS)
    alibi = -slopes[:, None, None] * dist.astype(F32)[None] * LOG2E
    variants = []
    for block in range(SWA_BIAS_VARIANTS):
        valid = in_band & ((block - 1) * BLOCK + col >= PAD)
        variants.append(jnp.where(valid[None], alibi, NEG))
    return jnp.stack(variants)


def _swa(sinks, bias, sq, skd, svd, tm):
    bsz, lp, _ = sq.shape
    n_sub = tm // BLOCK
    cur = lambda w: pl.BlockSpec((None, tm, w), lambda bi, i: (bi, i, 0))
    prev = lambda w: pl.BlockSpec((None, BLOCK, w), lambda bi, i: (bi, jnp.maximum(i * n_sub - 1, 0), 0))
    return pl.pallas_call(
        functools.partial(_swa_kernel, n_sub=n_sub),
        grid=(bsz, lp // tm),
        in_specs=[pl.BlockSpec(memory_space=pltpu.SMEM),
                  pl.BlockSpec(bias.shape, lambda bi, i: (0, 0, 0, 0), pipeline_mode=pl.Buffered(1)),
                  cur(MIX_B), prev(2 * LANES), cur(2 * LANES), prev(2 * LANES), cur(2 * LANES)],
        out_specs=cur(MIX_B),
        out_shape=jax.ShapeDtypeStruct((bsz, lp, MIX_B), BF16),
        name="swa",
    )(sinks, bias, sq, skd, skd, svd, svd)


def _mix_ffn_kernel(*refs, n_in):
    rows = lambda k: refs[k * n_in:(k + 1) * n_in]
    (ga_ref, gb_ref, wo_ref, g1_ref, b1_ref, wg_ref, wu_ref, wd_ref, g2_ref, b2_ref, o_ref) = refs[3 * n_in:]
    tile = lambda rs: jnp.concatenate([r[...] for r in rs], axis=0) if n_in > 1 else rs[0][...]
    a = _rms_norm(tile(rows(0)).astype(F32), ga_ref[...]).astype(BF16)
    b = _rms_norm(tile(rows(1)).astype(F32), gb_ref[...]).astype(BF16)
    y = _dot(jnp.concatenate([a, b], axis=-1), wo_ref[...])
    x = _layer_norm(ALPHA * tile(rows(2)) + y, g1_ref[...], b1_ref[...])
    xb = x.astype(BF16)
    gate = _dot(xb, wg_ref[...])
    up = _dot(xb, wu_ref[...])
    act = (gate * (1.0 / (1.0 + jnp.exp(-gate))) * up).astype(BF16)
    f = _dot(act, wd_ref[...])
    o_ref[...] = _layer_norm(ALPHA * x + f, g2_ref[...], b2_ref[...])


def _mix_ffn_vmem_bytes(tm, d):
    weights = 2 * ((MIX_A + MIX_B) * d + d * 2 * D_FF + D_FF * d)
    row_tiles = 2 * (tm * (MIX_A + MIX_B) * 2 + 2 * tm * d * 4)
    temporaries = tm * (2 * D_FF * 4 + D_FF * 2 + 4 * d * 4)
    return weights + row_tiles + temporaries


def _mix_ffn(o_a, o_b, h, layer, ga, gb, w_out, g1, b1, w_gate_up, w_down, g2, b2, tm, final):
    bsz, lp, d = h.shape
    spec = lambda a: _layer_spec(a, layer)
    param_specs = [spec(ga), spec(gb), spec(w_out), spec(g1), spec(b1),
                   _layer_spec(w_gate_up, layer, 0, 2), _layer_spec(w_gate_up, layer, 1, 2),
                   spec(w_down), spec(g2), spec(b2)]
    params = (ga, gb, w_out, g1, b1, w_gate_up, w_gate_up, w_down, g2, b2)
    if final:
        rows, n_in = lp - BLOCK, tm // BLOCK
        row_specs = lambda w: [pl.BlockSpec((None, BLOCK, w), lambda bi, i, t=t: (bi, 1 + i * n_in + t, 0))
                               for t in range(n_in)]
    else:
        rows, n_in = lp, 1
        row_specs = lambda w: [pl.BlockSpec((None, tm, w), lambda bi, i: (bi, i, 0))]
    return pl.pallas_call(
        functools.partial(_mix_ffn_kernel, n_in=n_in),
        grid=(bsz, rows // tm),
        in_specs=row_specs(MIX_A) + row_specs(MIX_B) + row_specs(d) + param_specs,
        out_specs=pl.BlockSpec((None, tm, d), lambda bi, i: (bi, i, 0)),
        out_shape=jax.ShapeDtypeStruct((bsz, rows, d), F32),
        compiler_params=pltpu.CompilerParams(vmem_limit_bytes=_mix_ffn_vmem_bytes(tm, d)),
        name="mix_ffn",
    )(*([o_a] * n_in), *([o_b] * n_in), *([h] * n_in), *params)


def _rope_tables(lp):
    pos = jnp.maximum(jnp.arange(lp, dtype=jnp.int32) - PAD, 0).astype(F32)
    inv = ROPE_THETA ** (-jnp.arange(0, MLA_ROPE, 2, dtype=F32) / MLA_ROPE)
    ang = pos[:, None] * inv[None, :]
    cos, sin = jnp.cos(ang), jnp.sin(ang)
    z16 = jnp.zeros((lp, ROPE_HALF), F32)
    z32 = jnp.zeros((lp, LANES - MLA_NOPE - MLA_ROPE), F32)
    one = jnp.ones((lp, MLA_NOPE), F32)
    z64 = jnp.zeros((lp, MLA_NOPE), F32)
    c = jnp.concatenate([one, cos, cos, z32], -1)
    sa = jnp.concatenate([z64, -sin, z16, z32], -1)
    sb = jnp.concatenate([z64, z16, sin, z32], -1)
    return c, sa, sb


def _layout_w_in(w):
    w = w.astype(BF16)
    o = 0
    q_c = w[..., o:o + MLA_Q_RANK]; o += MLA_Q_RANK
    kv_c = w[..., o:o + MLA_KV_RANK]; o += MLA_KV_RANK
    k_r = w[..., o:o + MLA_ROPE]; o += MLA_ROPE
    rest = w[..., o:]
    k_r_p = jnp.concatenate([jnp.zeros(w.shape[:-1] + (MLA_NOPE,), BF16), k_r,
                             jnp.zeros(w.shape[:-1] + (LANES - MLA_NOPE - MLA_ROPE,), BF16)], -1)
    return jnp.concatenate([q_c, kv_c, k_r_p, rest], -1)


def _layout_w_uq(w):
    n = w.shape[0]
    w = w.astype(BF16).reshape(n, MLA_Q_RANK, MLA_HEADS, MLA_NOPE + MLA_ROPE)
    w = jnp.pad(w, ((0, 0), (0, 0), (0, 0), (0, HEAD_PAD - MLA_NOPE - MLA_ROPE)))
    return w.reshape(n, MLA_Q_RANK, MLA_HEADS * HEAD_PAD)


def _layout_w_ukv(w):
    n = w.shape[0]
    w = w.astype(BF16).reshape(n, MLA_KV_RANK, MLA_HEADS, MLA_NOPE + MLA_V)
    k = jnp.pad(w[..., :MLA_NOPE], ((0, 0), (0, 0), (0, 0), (0, HEAD_PAD - MLA_NOPE)))
    v = w[..., MLA_NOPE:]
    return jnp.concatenate([k.reshape(n, MLA_KV_RANK, -1), v.reshape(n, MLA_KV_RANK, -1)], -1)


def kernel(x, meta_tokens, ln_in_g, ln_in_b, w_in, q_norm_g, w_uq, kv_norm_g, w_ukv, attn_sinks,
           grp_norm_a, grp_norm_b, w_out, ln1_g, ln1_b, w_gate_up, w_down, ln2_g, ln2_b):
    bsz, s, d = x.shape
    assert d == D_MODEL and s % ATT_TILE == 0
    lp = s + BLOCK
    tm = _row_tile(lp)
    vec = lambda a: a.reshape(1, -1)

    meta_pad = jnp.pad(meta_tokens.astype(x.dtype), ((PAD, 0), (0, 0)))
    h = _ln_in(x, meta_pad, vec(ln_in_g), vec(ln_in_b), tm)
    c, sa, sb = _rope_tables(lp)
    swa_bias = _swa_bias()

    vecs = lambda a: a[:, None, :]
    w_in_p, w_uq_p, w_ukv_p = _layout_w_in(w_in), _layout_w_uq(w_uq), _layout_w_ukv(w_ukv)
    w_out_b, w_gate_up_b, w_down_b = w_out.astype(BF16), w_gate_up.astype(BF16), w_down.astype(BF16)

    for l in range(DEPTH):
        q, k, v, sq, skd, svd = _proj(h, l, w_in_p, vecs(q_norm_g), w_uq_p, vecs(kv_norm_g), w_ukv_p,
                                      c, sa, sb, tm)
        o_a = _mla(q, k, v)
        o_b = _swa(attn_sinks[l], swa_bias, sq, skd, svd, tm)
        final = l == DEPTH - 1
        h = _mix_ffn(o_a, o_b, h, l, vecs(grp_norm_a), vecs(grp_norm_b), w_out_b, vecs(ln1_g), vecs(ln1_b),
                     w_gate_up_b, w_down_b, vecs(ln2_g), vecs(ln2_b), FINAL_ROW_TILE if final else tm, final)
    return h
```

```python
import functools

import jax
import jax.numpy as jnp
import numpy as np
from jax import lax
from jax.experimental import pallas as pl
from jax.experimental.pallas import tpu as pltpu

D_MODEL = 1024
DEPTH = 4
N_META = 16
BLOCK = 128
PAD = BLOCK - N_META
MLA_HEADS = 8
MLA_Q_RANK = 256
MLA_KV_RANK = 128
MLA_NOPE = 64
MLA_ROPE = 32
MLA_V = 64
ROPE_THETA = 10000.0
SWA_HEADS = 8
SWA_KV_HEADS = 2
SWA_GROUP = SWA_HEADS // SWA_KV_HEADS
SWA_HEAD_DIM = 64
WINDOW = 128
MIX_A = MLA_HEADS * MLA_V
MIX_B = SWA_HEADS * SWA_HEAD_DIM
D_FF = 2816
ALPHA = (2 * DEPTH) ** 0.25
LN_EPS = 1e-5
RMS_EPS = 1e-6
NEG = -1e30

LANES = 128
HEAD_PAD = LANES
ROPE_HALF = MLA_ROPE // 2
ATT_TILE = 512
FINAL_ROW_TILE = ATT_TILE
LOG2E = float(np.log2(np.e))
SWA_BIAS_VARIANTS = 3

BF16 = jnp.bfloat16
F32 = jnp.float32


LN_TILE_MAX_ROWS = 1536


def _ln_tile(lp):
    n_blocks = lp // BLOCK
    return BLOCK * max(k for k in range(1, LN_TILE_MAX_ROWS // BLOCK + 1) if n_blocks % k == 0)


def _row_tile(lp):
    for t in (512, 384, 256, 128):
        if lp % t == 0:
            return t
    raise ValueError(f"padded length {lp} is not a multiple of {BLOCK}")


def _layer_norm(z, g, b):
    mu = jnp.mean(z, -1, keepdims=True)
    zc = z - mu
    var = jnp.mean(zc * zc, -1, keepdims=True)
    return zc * lax.rsqrt(var + LN_EPS) * g + b


def _rms_norm(z, g):
    return z * lax.rsqrt(jnp.mean(z * z, -1, keepdims=True) + RMS_EPS) * g


def _dot(a, b):
    return jnp.dot(a, b, preferred_element_type=F32)


def _dot_nt(a, b):
    return lax.dot_general(a, b, (((1,), (1,)), ((), ())), preferred_element_type=F32)


def _ln_in_kernel(*refs, n_sub):
    x_refs, (meta_ref, g_ref, b_ref, o_ref) = refs[:n_sub], refs[n_sub:]
    i = pl.program_id(1)
    for t, x_ref in enumerate(x_refs):
        z = x_ref[...]
        if t == 0:
            z = jnp.where(i == 0, meta_ref[...], z)
        o_ref[t * BLOCK:(t + 1) * BLOCK, :] = _layer_norm(z, g_ref[...], b_ref[...])


def _ln_in(x, meta_pad, g, b, tm):
    bsz, s, d = x.shape
    lp = s + BLOCK
    n_sub = tm // BLOCK
    x_spec = lambda t: pl.BlockSpec((None, BLOCK, d), lambda bi, i: (bi, jnp.maximum(i * n_sub + t - 1, 0), 0))
    full = lambda a: pl.BlockSpec(a.shape, lambda bi, i: (0,) * a.ndim)
    return pl.pallas_call(
        functools.partial(_ln_in_kernel, n_sub=n_sub),
        grid=(bsz, lp // tm),
        in_specs=[x_spec(t) for t in range(n_sub)] + [full(meta_pad), full(g), full(b)],
        out_specs=pl.BlockSpec((None, tm, d), lambda bi, i: (bi, i, 0)),
        out_shape=jax.ShapeDtypeStruct((bsz, lp, d), F32),
        name="ln_in",
    )(*([x] * n_sub), meta_pad, g, b)


def _rope(z, c, sa, sb):
    return z * c + pltpu.roll(z, LANES - ROPE_HALF, 1) * sa + pltpu.roll(z, ROPE_HALF, 1) * sb


def _proj_kernel(h_ref, w_in_ref, gq_ref, w_uq_ref, gkv_ref, w_ukv_ref, c_ref, sa_ref, sb_ref,
                 qcos_ref, qsin_ref, q_ref, k_ref, v_ref, sq_ref, sk_ref, sv_ref):
    x = h_ref[...].astype(BF16)
    proj = _dot(x, w_in_ref[...])
    c, sa, sb = c_ref[...], sa_ref[...], sb_ref[...]
    o = 0
    q_c = proj[:, o:o + MLA_Q_RANK]; o += MLA_Q_RANK
    kv_c = proj[:, o:o + MLA_KV_RANK]; o += MLA_KV_RANK
    k_r = proj[:, o:o + LANES]; o += LANES
    s_q = proj[:, o:o + MIX_B]; o += MIX_B
    s_k = proj[:, o:o + LANES]; o += LANES
    s_v = proj[:, o:o + LANES]

    q = _dot(_rms_norm(q_c, gq_ref[...]).astype(BF16), w_uq_ref[...])
    qcos, qsin = qcos_ref[...], qsin_ref[...]
    n_q = MLA_HEADS * HEAD_PAD
    for h in range(MLA_HEADS):
        sl = slice(h * HEAD_PAD, (h + 1) * HEAD_PAD)
        swapped = slice(n_q + h * HEAD_PAD, n_q + (h + 1) * HEAD_PAD)
        q_ref[:, sl] = (q[:, sl] * qcos + q[:, swapped] * qsin).astype(BF16)

    kv = _dot(_rms_norm(kv_c, gkv_ref[...]).astype(BF16), w_ukv_ref[...])
    k_rope = _rope(k_r, c, sa, sb)
    for h in range(MLA_HEADS):
        sl = slice(h * HEAD_PAD, (h + 1) * HEAD_PAD)
        k_ref[:, sl] = (kv[:, sl] + k_rope).astype(BF16)
    v_ref[...] = kv[:, MLA_HEADS * HEAD_PAD:].astype(BF16)

    sq_ref[...] = (s_q * (SWA_HEAD_DIM ** -0.5 * LOG2E)).astype(BF16)
    low = lax.broadcasted_iota(jnp.int32, s_k.shape, 1) < SWA_HEAD_DIM
    for src, dst in ((s_k, sk_ref), (s_v, sv_ref)):
        swapped = pltpu.roll(src, SWA_HEAD_DIM, 1)
        dst[:, :LANES] = jnp.where(low, src, swapped).astype(BF16)
        dst[:, LANES:] = jnp.where(low, swapped, src).astype(BF16)


def _layer_spec(a, layer, col_block=0, n_col_blocks=1):
    _, rows, cols = a.shape
    return pl.BlockSpec((None, rows, cols // n_col_blocks), lambda bi, i: (layer, 0, col_block),
                        pipeline_mode=pl.Buffered(1))


def _proj(h, layer, w_in_p, gq, w_uq_p, gkv, w_ukv_p, tables, tm):
    bsz, lp, d = h.shape
    nt = lp // tm
    row = lambda w: pl.BlockSpec((None, tm, w), lambda bi, i: (bi, i, 0))
    tab = pl.BlockSpec((tm, LANES), lambda bi, i: (i, 0))
    params = (w_in_p, gq, w_uq_p, gkv, w_ukv_p)
    widths = (MLA_HEADS * HEAD_PAD, MLA_HEADS * HEAD_PAD, MIX_A, MIX_B, 2 * LANES, 2 * LANES)
    return pl.pallas_call(
        _proj_kernel,
        grid=(bsz, nt),
        in_specs=[row(d)] + [_layer_spec(p, layer) for p in params] + [tab] * len(tables),
        out_specs=[row(w) for w in widths],
        out_shape=[jax.ShapeDtypeStruct((bsz, lp, w), BF16) for w in widths],
        name="proj",
    )(h, *params, *tables)


def _mla_schedule(n_tiles):
    rows = []
    for j in range(n_tiles):
        for c in range(j + 1):
            rows.append((BLOCK + j * ATT_TILE, BLOCK + c * ATT_TILE, j, int(c == 0), int(c == j)))
    return np.asarray(rows, np.int32)


def _mla_kernel(sched_ref, q_ref, k_ref, v_ref, o_ref,
                s_ref, p_ref, alpha_ref, m_ref, acc_ref, meta_m_ref, meta_acc_ref, *, n_tiles):
    n_steps = n_tiles * (n_tiles + 1) // 2
    lane = lax.broadcasted_iota(jnp.int32, (1, 2 * LANES), 1)
    head_a = (lane % LANES) < MLA_V
    tr = lax.broadcasted_iota(jnp.int32, (ATT_TILE, ATT_TILE), 0)
    tc = lax.broadcasted_iota(jnp.int32, (ATT_TILE, ATT_TILE), 1)
    diag_mask = tc <= tr

    def sched(t, col):
        return sched_ref[t, col]

    def rows_at(off):
        return pl.ds(pl.multiple_of(off, BLOCK), ATT_TILE)

    def value_matrix(vc):
        n = vc.shape[0]
        lane_i = lax.broadcasted_iota(jnp.int32, (n, LANES), 1)
        low = lane_i < MLA_V
        zero = jnp.zeros((n, LANES), BF16)
        one_lo = jnp.clip(MLA_V - lane_i, 0, 1).astype(F32)
        top = jnp.concatenate([jnp.where(low, vc, zero), one_lo.astype(BF16)], axis=1)
        bot = jnp.concatenate([jnp.where(low, zero, vc), (1.0 - one_lo).astype(BF16)], axis=1)
        return jnp.concatenate([top, bot], axis=0)

    def heads(x):
        return x[:, :HEAD_PAD], x[:, HEAD_PAD:]

    def normalized(acc):
        return (acc[:, :LANES] / acc[:, LANES:]).astype(BF16)

    def meta_attention(q, mask):
        ms, ps = [], []
        for qh, kh in zip(heads(q), heads(k_ref[:BLOCK, :])):
            s = jnp.where(mask, _dot_nt(qh, kh), NEG)
            m = jnp.max(s, -1, keepdims=True)
            ms.append(m)
            ps.append(jnp.exp2(s - m).astype(BF16))
        return ms, _dot(jnp.concatenate(ps, axis=1), value_matrix(v_ref[:BLOCK, :]))

    def score_stage(t, slot):
        q, kc = q_ref[rows_at(sched(t, 0)), :], k_ref[rows_at(sched(t, 1)), :]
        for h, (qh, kh) in enumerate(zip(heads(q), heads(kc))):
            s_ref[slot, h] = _dot_nt(qh, kh)

    def mask_stage(t, slot):
        @pl.when(sched(t, 4) == 1)
        def _():
            for h in range(2):
                s_ref[slot, h] = jnp.where(diag_mask, s_ref[slot, h], NEG)

    def softmax_stage(t, slot):
        first = sched(t, 3) == 1
        alphas = []
        for h in range(2):
            s = s_ref[slot, h]
            m_prev = jnp.where(first, meta_m_ref[sched(t, 2), h], m_ref[h])
            m_new = jnp.maximum(m_prev, jnp.max(s, -1, keepdims=True))
            alphas.append(jnp.exp2(m_prev - m_new))
            m_ref[h] = m_new
            p_ref[slot, :, h * ATT_TILE:(h + 1) * ATT_TILE] = jnp.exp2(s - m_new).astype(BF16)
        alpha_ref[slot] = jnp.where(head_a, alphas[0], alphas[1])

    def value_stage(t, slot):
        first = sched(t, 3) == 1
        pv = _dot(p_ref[slot], value_matrix(v_ref[rows_at(sched(t, 1)), :]))
        base = jnp.where(first, meta_acc_ref[sched(t, 2)], acc_ref[...])
        acc_ref[...] = alpha_ref[slot] * base + pv

    def output_stage(t):
        @pl.when(sched(t, 4) == 1)
        def _():
            o_ref[rows_at(sched(t, 0)), :] = normalized(acc_ref[...])

    def step(t, slot):
        score_stage(t + 1, 1 - slot)
        value_stage(t - 1, 1 - slot)
        softmax_stage(t, slot)
        output_stage(t - 1)
        mask_stage(t + 1, 1 - slot)

    r = lax.broadcasted_iota(jnp.int32, (BLOCK, BLOCK), 0)
    cidx = lax.broadcasted_iota(jnp.int32, (BLOCK, BLOCK), 1)
    _, acc0 = meta_attention(q_ref[:BLOCK, :], (cidx <= r) & (cidx >= PAD))
    o_ref[:BLOCK, :] = normalized(acc0)

    meta_mask = lax.broadcasted_iota(jnp.int32, (ATT_TILE, BLOCK), 1) >= PAD
    for j in range(n_tiles):
        ms, acc = meta_attention(q_ref[BLOCK + j * ATT_TILE:BLOCK + (j + 1) * ATT_TILE, :], meta_mask)
        meta_acc_ref[j] = acc
        for h in range(2):
            meta_m_ref[j, h] = ms[h]

    score_stage(0, 0)
    mask_stage(0, 0)
    if n_steps == 1:
        softmax_stage(0, 0)
    else:
        score_stage(1, 1)
        softmax_stage(0, 0)
        mask_stage(1, 1)
        n_mid = n_steps - 2

        def two_steps(i, carry):
            t = 1 + 2 * i
            step(t, 1)
            step(t + 1, 0)
            return carry

        lax.fori_loop(0, n_mid // 2, two_steps, 0)
        if n_mid % 2:
            step(n_steps - 2, (n_steps - 2) % 2)
        last = n_steps - 1
        value_stage(last - 1, (last - 1) % 2)
        softmax_stage(last, last % 2)
        output_stage(last - 1)
    value_stage(n_steps - 1, (n_steps - 1) % 2)
    output_stage(n_steps - 1)


def _mla(q, k, v):
    bsz, lp, _ = q.shape
    n_pairs = MLA_HEADS // 2
    n_tiles = (lp - BLOCK) // ATT_TILE
    sched = jnp.asarray(_mla_schedule(n_tiles))
    qk_spec = pl.BlockSpec((None, lp, 2 * HEAD_PAD), lambda bi, p: (bi, 0, p))
    v_spec = pl.BlockSpec((None, lp, 2 * MLA_V), lambda bi, p: (bi, 0, p))
    return pl.pallas_call(
        functools.partial(_mla_kernel, n_tiles=n_tiles),
        grid=(bsz, n_pairs),
        in_specs=[pl.BlockSpec(memory_space=pltpu.SMEM), qk_spec, qk_spec, v_spec],
        out_specs=v_spec,
        out_shape=jax.ShapeDtypeStruct((bsz, lp, MIX_A), BF16),
        scratch_shapes=[
            pltpu.VMEM((2, 2, ATT_TILE, ATT_TILE), F32),
            pltpu.VMEM((2, ATT_TILE, 2 * ATT_TILE), BF16),
            pltpu.VMEM((2, ATT_TILE, 2 * LANES), F32),
            pltpu.VMEM((2, ATT_TILE, 1), F32),
            pltpu.VMEM((ATT_TILE, 2 * LANES), F32),
            pltpu.VMEM((n_tiles, 2, ATT_TILE, 1), F32),
            pltpu.VMEM((n_tiles, ATT_TILE, 2 * LANES), F32),
        ],
        name="mla",
    )(sched, q, k, v)


def _swa_kernel(sinks_ref, bias_ref, q_ref, kp_ref, kc_ref, vp_ref, vc_ref, o_ref, *, n_sub):
    first_block = pl.program_id(1) * n_sub
    low = lax.broadcasted_iota(jnp.int32, (BLOCK, LANES), 1) < SWA_HEAD_DIM
    zero = jnp.zeros((BLOCK, LANES), BF16)

    for t in range(n_sub):
        rows = slice(t * BLOCK, (t + 1) * BLOCK)
        variant = jnp.minimum(first_block + t, SWA_BIAS_VARIANTS - 1)
        for kvh in range(SWA_KV_HEADS):
            sl = slice(kvh * LANES, (kvh + 1) * LANES)
            if t == 0:
                k_band = jnp.concatenate([kp_ref[:, sl], kc_ref[rows, sl]], axis=0)
                v_band = jnp.concatenate([vp_ref[:, sl], vc_ref[rows, sl]], axis=0)
            else:
                band = slice((t - 1) * BLOCK, (t + 1) * BLOCK)
                k_band, v_band = kc_ref[band, sl], vc_ref[band, sl]
            q_rows = []
            for pair in range(SWA_GROUP // 2):
                g = kvh * (SWA_GROUP // 2) + pair
                qp = q_ref[rows, g * LANES:(g + 1) * LANES]
                q_rows += [jnp.where(low, qp, zero), jnp.where(low, zero, qp)]
            s_all = _dot_nt(jnp.concatenate(q_rows, axis=0), k_band)
            probs = []
            for gi in range(SWA_GROUP):
                head = kvh * SWA_GROUP + gi
                sink = sinks_ref[head] * LOG2E
                s = s_all[gi * BLOCK:(gi + 1) * BLOCK] + bias_ref[variant, head]
                m = jnp.maximum(jnp.max(s, -1, keepdims=True), sink)
                e = jnp.exp2(s - m)
                p = e / (jnp.sum(e, -1, keepdims=True) + jnp.exp2(sink - m))
                probs.append(p.astype(BF16))
            o_all = _dot(jnp.concatenate(probs, axis=0), v_band)
            for pair in range(SWA_GROUP // 2):
                g = kvh * (SWA_GROUP // 2) + pair
                o_lo = o_all[(2 * pair) * BLOCK:(2 * pair + 1) * BLOCK]
                o_hi = o_all[(2 * pair + 1) * BLOCK:(2 * pair + 2) * BLOCK]
                o_ref[rows, g * LANES:(g + 1) * LANES] = jnp.where(low, o_lo, o_hi).astype(BF16)


def _swa_bias():
    row = jnp.arange(BLOCK)[:, None]
    col = jnp.arange(2 * BLOCK)[None, :]
    dist = BLOCK + row - col
    in_band = (dist >= 0) & (dist < WINDOW)
    slopes = jnp.exp2(-8.0 * jnp.arange(1, SWA_HEADS + 1, dtype=F32) / SWA_HEADS)
    alibi = -slopes[:, None, None] * dist.astype(F32)[None] * LOG2E
    variants = []
    for block in range(SWA_BIAS_VARIANTS):
        valid = in_band & ((block - 1) * BLOCK + col >= PAD)
        variants.append(jnp.where(valid[None], alibi, NEG))
    return jnp.stack(variants)


def _swa(sinks, bias, sq, skd, svd, tm):
    bsz, lp, _ = sq.shape
    n_sub = tm // BLOCK
    cur = lambda w: pl.BlockSpec((None, tm, w), lambda bi, i: (bi, i, 0))
    prev = lambda w: pl.BlockSpec((None, BLOCK, w), lambda bi, i: (bi, jnp.maximum(i * n_sub - 1, 0), 0))
    return pl.pallas_call(
        functools.partial(_swa_kernel, n_sub=n_sub),
        grid=(bsz, lp // tm),
        in_specs=[pl.BlockSpec(memory_space=pltpu.SMEM),
                  pl.BlockSpec(bias.shape, lambda bi, i: (0, 0, 0, 0), pipeline_mode=pl.Buffered(1)),
                  cur(MIX_B), prev(2 * LANES), cur(2 * LANES), prev(2 * LANES), cur(2 * LANES)],
        out_specs=cur(MIX_B),
        out_shape=jax.ShapeDtypeStruct((bsz, lp, MIX_B), BF16),
        name="swa",
    )(sinks, bias, sq, skd, skd, svd, svd)


def _mix_ffn_kernel(*refs, n_in):
    rows = lambda k: refs[k * n_in:(k + 1) * n_in]
    (ga_ref, gb_ref, wo_ref, g1_ref, b1_ref, wg_ref, wu_ref, wd_ref, g2_ref, b2_ref, o_ref) = refs[3 * n_in:]
    tile = lambda rs: jnp.concatenate([r[...] for r in rs], axis=0) if n_in > 1 else rs[0][...]
    a = _rms_norm(tile(rows(0)).astype(F32), ga_ref[...]).astype(BF16)
    b = _rms_norm(tile(rows(1)).astype(F32), gb_ref[...]).astype(BF16)
    y = _dot(jnp.concatenate([a, b], axis=-1), wo_ref[...])
    x = _layer_norm(ALPHA * tile(rows(2)) + y, g1_ref[...], b1_ref[...])
    xb = x.astype(BF16)
    gate = _dot(xb, wg_ref[...])
    up = _dot(xb, wu_ref[...])
    act = (gate * (1.0 / (1.0 + jnp.exp(-gate))) * up).astype(BF16)
    f = _dot(act, wd_ref[...])
    o_ref[...] = _layer_norm(ALPHA * x + f, g2_ref[...], b2_ref[...])


def _mix_ffn_vmem_bytes(tm, d):
    weights = 2 * ((MIX_A + MIX_B) * d + d * 2 * D_FF + D_FF * d)
    row_tiles = 2 * (tm * (MIX_A + MIX_B) * 2 + 2 * tm * d * 4)
    temporaries = tm * (2 * D_FF * 4 + D_FF * 2 + 4 * d * 4)
    return weights + row_tiles + temporaries


def _mix_ffn(o_a, o_b, h, layer, ga, gb, w_out, g1, b1, w_gate_up, w_down, g2, b2, tm, final):
    bsz, lp, d = h.shape
    spec = lambda a: _layer_spec(a, layer)
    param_specs = [spec(ga), spec(gb), spec(w_out), spec(g1), spec(b1),
                   _layer_spec(w_gate_up, layer, 0, 2), _layer_spec(w_gate_up, layer, 1, 2),
                   spec(w_down), spec(g2), spec(b2)]
    params = (ga, gb, w_out, g1, b1, w_gate_up, w_gate_up, w_down, g2, b2)
    if final:
        rows, n_in = lp - BLOCK, tm // BLOCK
        row_specs = lambda w: [pl.BlockSpec((None, BLOCK, w), lambda bi, i, t=t: (bi, 1 + i * n_in + t, 0))
                               for t in range(n_in)]
    else:
        rows, n_in = lp, 1
        row_specs = lambda w: [pl.BlockSpec((None, tm, w), lambda bi, i: (bi, i, 0))]
    return pl.pallas_call(
        functools.partial(_mix_ffn_kernel, n_in=n_in),
        grid=(bsz, rows // tm),
        in_specs=row_specs(MIX_A) + row_specs(MIX_B) + row_specs(d) + param_specs,
        out_specs=pl.BlockSpec((None, tm, d), lambda bi, i: (bi, i, 0)),
        out_shape=jax.ShapeDtypeStruct((bsz, rows, d), F32),
        compiler_params=pltpu.CompilerParams(vmem_limit_bytes=_mix_ffn_vmem_bytes(tm, d)),
        name="mix_ffn",
    )(*([o_a] * n_in), *([o_b] * n_in), *([h] * n_in), *params)


def _rope_tables(lp):
    pos = jnp.maximum(jnp.arange(lp, dtype=jnp.int32) - PAD, 0).astype(F32)
    inv = ROPE_THETA ** (-jnp.arange(0, MLA_ROPE, 2, dtype=F32) / MLA_ROPE)
    ang = pos[:, None] * inv[None, :]
    cos, sin = jnp.cos(ang), jnp.sin(ang)
    z16 = jnp.zeros((lp, ROPE_HALF), F32)
    z32 = jnp.zeros((lp, LANES - MLA_NOPE - MLA_ROPE), F32)
    one = jnp.ones((lp, MLA_NOPE), F32)
    z64 = jnp.zeros((lp, MLA_NOPE), F32)
    c = jnp.concatenate([one, cos, cos, z32], -1)
    sa = jnp.concatenate([z64, -sin, z16, z32], -1)
    sb = jnp.concatenate([z64, z16, sin, z32], -1)
    return c, sa, sb


def _layout_w_in(w):
    w = w.astype(BF16)
    o = 0
    q_c = w[..., o:o + MLA_Q_RANK]; o += MLA_Q_RANK
    kv_c = w[..., o:o + MLA_KV_RANK]; o += MLA_KV_RANK
    k_r = w[..., o:o + MLA_ROPE]; o += MLA_ROPE
    rest = w[..., o:]
    k_r_p = jnp.concatenate([jnp.zeros(w.shape[:-1] + (MLA_NOPE,), BF16), k_r,
                             jnp.zeros(w.shape[:-1] + (LANES - MLA_NOPE - MLA_ROPE,), BF16)], -1)
    return jnp.concatenate([q_c, kv_c, k_r_p, rest], -1)


def _layout_w_uq(w):
    n = w.shape[0]
    w = w.astype(BF16).reshape(n, MLA_Q_RANK, MLA_HEADS, MLA_NOPE + MLA_ROPE)
    x1, x2 = w[..., MLA_NOPE:MLA_NOPE + ROPE_HALF], w[..., MLA_NOPE + ROPE_HALF:]
    swapped = jnp.concatenate([jnp.zeros_like(w[..., :MLA_NOPE]), -x2, x1], -1)
    pad = ((0, 0), (0, 0), (0, 0), (0, HEAD_PAD - MLA_NOPE - MLA_ROPE))
    both = [jnp.pad(a, pad).reshape(n, MLA_Q_RANK, MLA_HEADS * HEAD_PAD) for a in (w, swapped)]
    return jnp.concatenate(both, -1)


def _layout_w_ukv(w):
    n = w.shape[0]
    w = w.astype(BF16).reshape(n, MLA_KV_RANK, MLA_HEADS, MLA_NOPE + MLA_V)
    k = jnp.pad(w[..., :MLA_NOPE], ((0, 0), (0, 0), (0, 0), (0, HEAD_PAD - MLA_NOPE)))
    v = w[..., MLA_NOPE:]
    return jnp.concatenate([k.reshape(n, MLA_KV_RANK, -1), v.reshape(n, MLA_KV_RANK, -1)], -1)


def kernel(x, meta_tokens, ln_in_g, ln_in_b, w_in, q_norm_g, w_uq, kv_norm_g, w_ukv, attn_sinks,
           grp_norm_a, grp_norm_b, w_out, ln1_g, ln1_b, w_gate_up, w_down, ln2_g, ln2_b):
    bsz, s, d = x.shape
    assert d == D_MODEL and s % ATT_TILE == 0
    lp = s + BLOCK
    tm = _row_tile(lp)
    vec = lambda a: a.reshape(1, -1)

    meta_pad = jnp.pad(meta_tokens.astype(x.dtype), ((PAD, 0), (0, 0)))
    h = _ln_in(x, meta_pad, vec(ln_in_g), vec(ln_in_b), _ln_tile(lp))
    c, sa, sb = _rope_tables(lp)
    q_scale = (MLA_NOPE + MLA_ROPE) ** -0.5 * LOG2E
    tables = (c, sa, sb, c * q_scale, (sb - sa) * q_scale)
    swa_bias = _swa_bias()

    vecs = lambda a: a[:, None, :]
    w_in_p, w_uq_p, w_ukv_p = _layout_w_in(w_in), _layout_w_uq(w_uq), _layout_w_ukv(w_ukv)
    w_out_b, w_gate_up_b, w_down_b = w_out.astype(BF16), w_gate_up.astype(BF16), w_down.astype(BF16)

    for l in range(DEPTH):
        q, k, v, sq, skd, svd = _proj(h, l, w_in_p, vecs(q_norm_g), w_uq_p, vecs(kv_norm_g), w_ukv_p,
                                      tables, tm)
        o_a = _mla(q, k, v)
        o_b = _swa(attn_sinks[l], swa_bias, sq, skd, svd, tm)
        final = l == DEPTH - 1
        h = _mix_ffn(o_a, o_b, h, l, vecs(grp_norm_a), vecs(grp_norm_b), w_out_b, vecs(ln1_g), vecs(ln1_b),
                     w_gate_up_b, w_down_b, vecs(ln2_g), vecs(ln2_b), FINAL_ROW_TILE if final else tm, final)
    return h
```

```python
import functools

import jax
import jax.numpy as jnp
import numpy as np
from jax import lax
from jax.experimental import pallas as pl
from jax.experimental.pallas import tpu as pltpu

D_MODEL = 1024
DEPTH = 4
N_META = 16
BLOCK = 128
PAD = BLOCK - N_META
MLA_HEADS = 8
MLA_Q_RANK = 256
MLA_KV_RANK = 128
MLA_NOPE = 64
MLA_ROPE = 32
MLA_V = 64
ROPE_THETA = 10000.0
SWA_HEADS = 8
SWA_KV_HEADS = 2
SWA_GROUP = SWA_HEADS // SWA_KV_HEADS
SWA_HEAD_DIM = 64
WINDOW = 128
MIX_A = MLA_HEADS * MLA_V
MIX_B = SWA_HEADS * SWA_HEAD_DIM
D_FF = 2816
ALPHA = (2 * DEPTH) ** 0.25
LN_EPS = 1e-5
RMS_EPS = 1e-6
NEG = -1e30

LANES = 128
HEAD_PAD = LANES
ROPE_HALF = MLA_ROPE // 2
ATT_TILE = 512
FINAL_ROW_TILE = ATT_TILE
LOG2E = float(np.log2(np.e))
SWA_BIAS_VARIANTS = 3

BF16 = jnp.bfloat16
F32 = jnp.float32


LN_TILE_MAX_ROWS = 1536


def _ln_tile(lp):
    n_blocks = lp // BLOCK
    return BLOCK * max(k for k in range(1, LN_TILE_MAX_ROWS // BLOCK + 1) if n_blocks % k == 0)


def _row_tile(lp):
    for t in (512, 384, 256, 128):
        if lp % t == 0:
            return t
    raise ValueError(f"padded length {lp} is not a multiple of {BLOCK}")


def _layer_norm(z, g, b):
    mu = jnp.mean(z, -1, keepdims=True)
    zc = z - mu
    var = jnp.mean(zc * zc, -1, keepdims=True)
    return zc * lax.rsqrt(var + LN_EPS) * g + b


def _rms_norm(z, g):
    return z * lax.rsqrt(jnp.mean(z * z, -1, keepdims=True) + RMS_EPS) * g


def _dot(a, b):
    return jnp.dot(a, b, preferred_element_type=F32)


def _dot_nt(a, b):
    return lax.dot_general(a, b, (((1,), (1,)), ((), ())), preferred_element_type=F32)


def _ln_in_kernel(*refs, n_sub):
    x_refs, (meta_ref, g_ref, b_ref, o_ref) = refs[:n_sub], refs[n_sub:]
    i = pl.program_id(1)
    for t, x_ref in enumerate(x_refs):
        z = x_ref[...]
        if t == 0:
            z = jnp.where(i == 0, meta_ref[...], z)
        o_ref[t * BLOCK:(t + 1) * BLOCK, :] = _layer_norm(z, g_ref[...], b_ref[...])


def _ln_in(x, meta_pad, g, b, tm):
    bsz, s, d = x.shape
    lp = s + BLOCK
    n_sub = tm // BLOCK
    x_spec = lambda t: pl.BlockSpec((None, BLOCK, d), lambda bi, i: (bi, jnp.maximum(i * n_sub + t - 1, 0), 0))
    full = lambda a: pl.BlockSpec(a.shape, lambda bi, i: (0,) * a.ndim)
    return pl.pallas_call(
        functools.partial(_ln_in_kernel, n_sub=n_sub),
        grid=(bsz, lp // tm),
        in_specs=[x_spec(t) for t in range(n_sub)] + [full(meta_pad), full(g), full(b)],
        out_specs=pl.BlockSpec((None, tm, d), lambda bi, i: (bi, i, 0)),
        out_shape=jax.ShapeDtypeStruct((bsz, lp, d), F32),
        name="ln_in",
    )(*([x] * n_sub), meta_pad, g, b)


def _rope(z, c, sa, sb):
    return z * c + pltpu.roll(z, LANES - ROPE_HALF, 1) * sa + pltpu.roll(z, ROPE_HALF, 1) * sb


def _proj_kernel(h_ref, w_in_ref, gq_ref, w_uq_ref, gkv_ref, w_ukv_ref, c_ref, sa_ref, sb_ref,
                 qcos_ref, qsin_ref, q_ref, k_ref, v_ref, sq_ref, sk_ref, sv_ref):
    x = h_ref[...].astype(BF16)
    proj = _dot(x, w_in_ref[...])
    c, sa, sb = c_ref[...], sa_ref[...], sb_ref[...]
    o = 0
    q_c = proj[:, o:o + MLA_Q_RANK]; o += MLA_Q_RANK
    kv_c = proj[:, o:o + MLA_KV_RANK]; o += MLA_KV_RANK
    k_r = proj[:, o:o + LANES]; o += LANES
    s_q = proj[:, o:o + MIX_B]; o += MIX_B
    s_k = proj[:, o:o + LANES]; o += LANES
    s_v = proj[:, o:o + LANES]

    q = _dot(_rms_norm(q_c, gq_ref[...]).astype(BF16), w_uq_ref[...])
    qcos, qsin = qcos_ref[...], qsin_ref[...]
    n_q = MLA_HEADS * HEAD_PAD
    for h in range(MLA_HEADS):
        sl = slice(h * HEAD_PAD, (h + 1) * HEAD_PAD)
        swapped = slice(n_q + h * HEAD_PAD, n_q + (h + 1) * HEAD_PAD)
        q_ref[:, sl] = (q[:, sl] * qcos + q[:, swapped] * qsin).astype(BF16)

    kv = _dot(_rms_norm(kv_c, gkv_ref[...]).astype(BF16), w_ukv_ref[...])
    k_rope = _rope(k_r, c, sa, sb)
    for h in range(MLA_HEADS):
        sl = slice(h * HEAD_PAD, (h + 1) * HEAD_PAD)
        k_ref[:, sl] = (kv[:, sl] + k_rope).astype(BF16)
    v_ref[...] = kv[:, MLA_HEADS * HEAD_PAD:].astype(BF16)

    sq_ref[...] = (s_q * (SWA_HEAD_DIM ** -0.5 * LOG2E)).astype(BF16)
    low = lax.broadcasted_iota(jnp.int32, s_k.shape, 1) < SWA_HEAD_DIM
    for src, dst in ((s_k, sk_ref), (s_v, sv_ref)):
        swapped = pltpu.roll(src, SWA_HEAD_DIM, 1)
        dst[:, :LANES] = jnp.where(low, src, swapped).astype(BF16)
        dst[:, LANES:] = jnp.where(low, swapped, src).astype(BF16)


def _layer_spec(a, layer, col_block=0, n_col_blocks=1):
    _, rows, cols = a.shape
    return pl.BlockSpec((None, rows, cols // n_col_blocks), lambda bi, i: (layer, 0, col_block),
                        pipeline_mode=pl.Buffered(1))


def _proj(h, layer, w_in_p, gq, w_uq_p, gkv, w_ukv_p, tables, tm):
    bsz, lp, d = h.shape
    nt = lp // tm
    row = lambda w: pl.BlockSpec((None, tm, w), lambda i, bi: (bi, i, 0))
    tab = pl.BlockSpec((tm, LANES), lambda i, bi: (i, 0))
    params = (w_in_p, gq, w_uq_p, gkv, w_ukv_p)
    widths = (MLA_HEADS * HEAD_PAD, MLA_HEADS * HEAD_PAD, MIX_A, MIX_B, 2 * LANES, 2 * LANES)
    return pl.pallas_call(
        _proj_kernel,
        grid=(nt, bsz),
        in_specs=[row(d)] + [_layer_spec(p, layer) for p in params] + [tab] * len(tables),
        out_specs=[row(w) for w in widths],
        out_shape=[jax.ShapeDtypeStruct((bsz, lp, w), BF16) for w in widths],
        name="proj",
    )(h, *params, *tables)


def _mla_schedule(n_tiles):
    rows = []
    for j in range(n_tiles):
        for c in range(j + 1):
            rows.append((BLOCK + j * ATT_TILE, BLOCK + c * ATT_TILE, j, int(c == 0), int(c == j)))
    return np.asarray(rows, np.int32)


def _mla_kernel(sched_ref, q_ref, k_ref, v_ref, o_ref,
                s_ref, p_ref, alpha_ref, m_ref, acc_ref, meta_m_ref, meta_acc_ref, *, n_tiles):
    n_steps = n_tiles * (n_tiles + 1) // 2
    lane = lax.broadcasted_iota(jnp.int32, (1, 2 * LANES), 1)
    head_a = (lane % LANES) < MLA_V
    tr = lax.broadcasted_iota(jnp.int32, (ATT_TILE, ATT_TILE), 0)
    tc = lax.broadcasted_iota(jnp.int32, (ATT_TILE, ATT_TILE), 1)
    diag_mask = tc <= tr

    def sched(t, col):
        return sched_ref[t, col]

    def rows_at(off):
        return pl.ds(pl.multiple_of(off, BLOCK), ATT_TILE)

    def value_matrix(vc):
        n = vc.shape[0]
        lane_i = lax.broadcasted_iota(jnp.int32, (n, LANES), 1)
        low = lane_i < MLA_V
        zero = jnp.zeros((n, LANES), BF16)
        one_lo = jnp.clip(MLA_V - lane_i, 0, 1).astype(F32)
        top = jnp.concatenate([jnp.where(low, vc, zero), one_lo.astype(BF16)], axis=1)
        bot = jnp.concatenate([jnp.where(low, zero, vc), (1.0 - one_lo).astype(BF16)], axis=1)
        return jnp.concatenate([top, bot], axis=0)

    def heads(x):
        return x[:, :HEAD_PAD], x[:, HEAD_PAD:]

    def normalized(acc):
        return (acc[:, :LANES] / acc[:, LANES:]).astype(BF16)

    def meta_attention(q, mask):
        ms, ps = [], []
        for qh, kh in zip(heads(q), heads(k_ref[:BLOCK, :])):
            s = jnp.where(mask, _dot_nt(qh, kh), NEG)
            m = jnp.max(s, -1, keepdims=True)
            ms.append(m)
            ps.append(jnp.exp2(s - m).astype(BF16))
        return ms, _dot(jnp.concatenate(ps, axis=1), value_matrix(v_ref[:BLOCK, :]))

    def score_stage(t, slot):
        q, kc = q_ref[rows_at(sched(t, 0)), :], k_ref[rows_at(sched(t, 1)), :]
        for h, (qh, kh) in enumerate(zip(heads(q), heads(kc))):
            s_ref[slot, h] = _dot_nt(qh, kh)

    def mask_stage(t, slot):
        @pl.when(sched(t, 4) == 1)
        def _():
            for h in range(2):
                s_ref[slot, h] = jnp.where(diag_mask, s_ref[slot, h], NEG)

    def softmax_stage(t, slot):
        first = sched(t, 3) == 1
        alphas = []
        for h in range(2):
            s = s_ref[slot, h]
            m_prev = jnp.where(first, meta_m_ref[sched(t, 2), h], m_ref[h])
            m_new = jnp.maximum(m_prev, jnp.max(s, -1, keepdims=True))
            alphas.append(jnp.exp2(m_prev - m_new))
            m_ref[h] = m_new
            p_ref[slot, :, h * ATT_TILE:(h + 1) * ATT_TILE] = jnp.exp2(s - m_new).astype(BF16)
        alpha_ref[slot] = jnp.where(head_a, alphas[0], alphas[1])

    def value_stage(t, slot):
        first = sched(t, 3) == 1
        pv = _dot(p_ref[slot], value_matrix(v_ref[rows_at(sched(t, 1)), :]))
        base = jnp.where(first, meta_acc_ref[sched(t, 2)], acc_ref[...])
        acc_ref[...] = alpha_ref[slot] * base + pv

    def output_stage(t):
        @pl.when(sched(t, 4) == 1)
        def _():
            o_ref[rows_at(sched(t, 0)), :] = normalized(acc_ref[...])

    def step(t, slot):
        score_stage(t + 1, 1 - slot)
        value_stage(t - 1, 1 - slot)
        softmax_stage(t, slot)
        output_stage(t - 1)
        mask_stage(t + 1, 1 - slot)

    r = lax.broadcasted_iota(jnp.int32, (BLOCK, BLOCK), 0)
    cidx = lax.broadcasted_iota(jnp.int32, (BLOCK, BLOCK), 1)
    _, acc0 = meta_attention(q_ref[:BLOCK, :], (cidx <= r) & (cidx >= PAD))
    o_ref[:BLOCK, :] = normalized(acc0)

    meta_mask = lax.broadcasted_iota(jnp.int32, (ATT_TILE, BLOCK), 1) >= PAD
    for j in range(n_tiles):
        ms, acc = meta_attention(q_ref[BLOCK + j * ATT_TILE:BLOCK + (j + 1) * ATT_TILE, :], meta_mask)
        meta_acc_ref[j] = acc
        for h in range(2):
            meta_m_ref[j, h] = ms[h]

    score_stage(0, 0)
    mask_stage(0, 0)
    if n_steps == 1:
        softmax_stage(0, 0)
    else:
        score_stage(1, 1)
        softmax_stage(0, 0)
        mask_stage(1, 1)
        n_mid = n_steps - 2

        def two_steps(i, carry):
            t = 1 + 2 * i
            step(t, 1)
            step(t + 1, 0)
            return carry

        lax.fori_loop(0, n_mid // 2, two_steps, 0)
        if n_mid % 2:
            step(n_steps - 2, (n_steps - 2) % 2)
        last = n_steps - 1
        value_stage(last - 1, (last - 1) % 2)
        softmax_stage(last, last % 2)
        output_stage(last - 1)
    value_stage(n_steps - 1, (n_steps - 1) % 2)
    output_stage(n_steps - 1)


def _mla(q, k, v):
    bsz, lp, _ = q.shape
    n_pairs = MLA_HEADS // 2
    n_tiles = (lp - BLOCK) // ATT_TILE
    sched = jnp.asarray(_mla_schedule(n_tiles))
    qk_spec = pl.BlockSpec((None, lp, 2 * HEAD_PAD), lambda bi, p: (bi, 0, p))
    v_spec = pl.BlockSpec((None, lp, 2 * MLA_V), lambda bi, p: (bi, 0, p))
    return pl.pallas_call(
        functools.partial(_mla_kernel, n_tiles=n_tiles),
        grid=(bsz, n_pairs),
        in_specs=[pl.BlockSpec(memory_space=pltpu.SMEM), qk_spec, qk_spec, v_spec],
        out_specs=v_spec,
        out_shape=jax.ShapeDtypeStruct((bsz, lp, MIX_A), BF16),
        scratch_shapes=[
            pltpu.VMEM((2, 2, ATT_TILE, ATT_TILE), F32),
            pltpu.VMEM((2, ATT_TILE, 2 * ATT_TILE), BF16),
            pltpu.VMEM((2, ATT_TILE, 2 * LANES), F32),
            pltpu.VMEM((2, ATT_TILE, 1), F32),
            pltpu.VMEM((ATT_TILE, 2 * LANES), F32),
            pltpu.VMEM((n_tiles, 2, ATT_TILE, 1), F32),
            pltpu.VMEM((n_tiles, ATT_TILE, 2 * LANES), F32),
        ],
        name="mla",
    )(sched, q, k, v)


def _swa_kernel(sinks_ref, bias_ref, q_ref, kp_ref, kc_ref, vp_ref, vc_ref, o_ref, *, n_sub):
    first_block = pl.program_id(1) * n_sub
    low = lax.broadcasted_iota(jnp.int32, (BLOCK, LANES), 1) < SWA_HEAD_DIM
    zero = jnp.zeros((BLOCK, LANES), BF16)

    for t in range(n_sub):
        rows = slice(t * BLOCK, (t + 1) * BLOCK)
        variant = jnp.minimum(first_block + t, SWA_BIAS_VARIANTS - 1)
        for kvh in range(SWA_KV_HEADS):
            sl = slice(kvh * LANES, (kvh + 1) * LANES)
            if t == 0:
                k_band = jnp.concatenate([kp_ref[:, sl], kc_ref[rows, sl]], axis=0)
                v_band = jnp.concatenate([vp_ref[:, sl], vc_ref[rows, sl]], axis=0)
            else:
                band = slice((t - 1) * BLOCK, (t + 1) * BLOCK)
                k_band, v_band = kc_ref[band, sl], vc_ref[band, sl]
            q_rows = []
            for pair in range(SWA_GROUP // 2):
                g = kvh * (SWA_GROUP // 2) + pair
                qp = q_ref[rows, g * LANES:(g + 1) * LANES]
                q_rows += [jnp.where(low, qp, zero), jnp.where(low, zero, qp)]
            s_all = _dot_nt(jnp.concatenate(q_rows, axis=0), k_band)
            probs = []
            for gi in range(SWA_GROUP):
                head = kvh * SWA_GROUP + gi
                sink = sinks_ref[head] * LOG2E
                s = s_all[gi * BLOCK:(gi + 1) * BLOCK] + bias_ref[variant, head]
                m = jnp.maximum(jnp.max(s, -1, keepdims=True), sink)
                e = jnp.exp2(s - m)
                p = e / (jnp.sum(e, -1, keepdims=True) + jnp.exp2(sink - m))
                probs.append(p.astype(BF16))
            o_all = _dot(jnp.concatenate(probs, axis=0), v_band)
            for pair in range(SWA_GROUP // 2):
                g = kvh * (SWA_GROUP // 2) + pair
                o_lo = o_all[(2 * pair) * BLOCK:(2 * pair + 1) * BLOCK]
                o_hi = o_all[(2 * pair + 1) * BLOCK:(2 * pair + 2) * BLOCK]
                o_ref[rows, g * LANES:(g + 1) * LANES] = jnp.where(low, o_lo, o_hi).astype(BF16)


def _swa_bias():
    row = jnp.arange(BLOCK)[:, None]
    col = jnp.arange(2 * BLOCK)[None, :]
    dist = BLOCK + row - col
    in_band = (dist >= 0) & (dist < WINDOW)
    slopes = jnp.exp2(-8.0 * jnp.arange(1, SWA_HEADS + 1, dtype=F32) / SWA_HEADS)
    alibi = -slopes[:, None, None] * dist.astype(F32)[None] * LOG2E
    variants = []
    for block in range(SWA_BIAS_VARIANTS):
        valid = in_band & ((block - 1) * BLOCK + col >= PAD)
        variants.append(jnp.where(valid[None], alibi, NEG))
    return jnp.stack(variants)


def _swa(sinks, bias, sq, skd, svd, tm):
    bsz, lp, _ = sq.shape
    n_sub = tm // BLOCK
    cur = lambda w: pl.BlockSpec((None, tm, w), lambda bi, i: (bi, i, 0))
    prev = lambda w: pl.BlockSpec((None, BLOCK, w), lambda bi, i: (bi, jnp.maximum(i * n_sub - 1, 0), 0))
    return pl.pallas_call(
        functools.partial(_swa_kernel, n_sub=n_sub),
        grid=(bsz, lp // tm),
        in_specs=[pl.BlockSpec(memory_space=pltpu.SMEM),
                  pl.BlockSpec(bias.shape, lambda bi, i: (0, 0, 0, 0), pipeline_mode=pl.Buffered(1)),
                  cur(MIX_B), prev(2 * LANES), cur(2 * LANES), prev(2 * LANES), cur(2 * LANES)],
        out_specs=cur(MIX_B),
        out_shape=jax.ShapeDtypeStruct((bsz, lp, MIX_B), BF16),
        name="swa",
    )(sinks, bias, sq, skd, skd, svd, svd)


def _mix_ffn_kernel(*refs, n_in):
    rows = lambda k: refs[k * n_in:(k + 1) * n_in]
    (ga_ref, gb_ref, wo_ref, g1_ref, b1_ref, wg_ref, wu_ref, wd_ref, g2_ref, b2_ref, o_ref) = refs[3 * n_in:]
    tile = lambda rs: jnp.concatenate([r[...] for r in rs], axis=0) if n_in > 1 else rs[0][...]
    a = _rms_norm(tile(rows(0)).astype(F32), ga_ref[...]).astype(BF16)
    b = _rms_norm(tile(rows(1)).astype(F32), gb_ref[...]).astype(BF16)
    y = _dot(jnp.concatenate([a, b], axis=-1), wo_ref[...])
    x = _layer_norm(ALPHA * tile(rows(2)) + y, g1_ref[...], b1_ref[...])
    xb = x.astype(BF16)
    gate = _dot(xb, wg_ref[...])
    up = _dot(xb, wu_ref[...])
    act = (gate * (1.0 / (1.0 + jnp.exp(-gate))) * up).astype(BF16)
    f = _dot(act, wd_ref[...])
    o_ref[...] = _layer_norm(ALPHA * x + f, g2_ref[...], b2_ref[...])


def _mix_ffn_vmem_bytes(tm, d):
    weights = 2 * ((MIX_A + MIX_B) * d + d * 2 * D_FF + D_FF * d)
    row_tiles = 2 * (tm * (MIX_A + MIX_B) * 2 + 2 * tm * d * 4)
    temporaries = tm * (2 * D_FF * 4 + D_FF * 2 + 4 * d * 4)
    return weights + row_tiles + temporaries


def _mix_ffn(o_a, o_b, h, layer, ga, gb, w_out, g1, b1, w_gate_up, w_down, g2, b2, tm, final):
    bsz, lp, d = h.shape
    spec = lambda a: _layer_spec(a, layer)
    param_specs = [spec(ga), spec(gb), spec(w_out), spec(g1), spec(b1),
                   _layer_spec(w_gate_up, layer, 0, 2), _layer_spec(w_gate_up, layer, 1, 2),
                   spec(w_down), spec(g2), spec(b2)]
    params = (ga, gb, w_out, g1, b1, w_gate_up, w_gate_up, w_down, g2, b2)
    if final:
        rows, n_in = lp - BLOCK, tm // BLOCK
        row_specs = lambda w: [pl.BlockSpec((None, BLOCK, w), lambda bi, i, t=t: (bi, 1 + i * n_in + t, 0))
                               for t in range(n_in)]
    else:
        rows, n_in = lp, 1
        row_specs = lambda w: [pl.BlockSpec((None, tm, w), lambda bi, i: (bi, i, 0))]
    return pl.pallas_call(
        functools.partial(_mix_ffn_kernel, n_in=n_in),
        grid=(bsz, rows // tm),
        in_specs=row_specs(MIX_A) + row_specs(MIX_B) + row_specs(d) + param_specs,
        out_specs=pl.BlockSpec((None, tm, d), lambda bi, i: (bi, i, 0)),
        out_shape=jax.ShapeDtypeStruct((bsz, rows, d), F32),
        compiler_params=pltpu.CompilerParams(vmem_limit_bytes=_mix_ffn_vmem_bytes(tm, d)),
        name="mix_ffn",
    )(*([o_a] * n_in), *([o_b] * n_in), *([h] * n_in), *params)


def _rope_tables(lp):
    pos = jnp.maximum(jnp.arange(lp, dtype=jnp.int32) - PAD, 0).astype(F32)
    inv = ROPE_THETA ** (-jnp.arange(0, MLA_ROPE, 2, dtype=F32) / MLA_ROPE)
    ang = pos[:, None] * inv[None, :]
    cos, sin = jnp.cos(ang), jnp.sin(ang)
    z16 = jnp.zeros((lp, ROPE_HALF), F32)
    z32 = jnp.zeros((lp, LANES - MLA_NOPE - MLA_ROPE), F32)
    one = jnp.ones((lp, MLA_NOPE), F32)
    z64 = jnp.zeros((lp, MLA_NOPE), F32)
    c = jnp.concatenate([one, cos, cos, z32], -1)
    sa = jnp.concatenate([z64, -sin, z16, z32], -1)
    sb = jnp.concatenate([z64, z16, sin, z32], -1)
    return c, sa, sb


def _layout_w_in(w):
    w = w.astype(BF16)
    o = 0
    q_c = w[..., o:o + MLA_Q_RANK]; o += MLA_Q_RANK
    kv_c = w[..., o:o + MLA_KV_RANK]; o += MLA_KV_RANK
    k_r = w[..., o:o + MLA_ROPE]; o += MLA_ROPE
    rest = w[..., o:]
    k_r_p = jnp.concatenate([jnp.zeros(w.shape[:-1] + (MLA_NOPE,), BF16), k_r,
                             jnp.zeros(w.shape[:-1] + (LANES - MLA_NOPE - MLA_ROPE,), BF16)], -1)
    return jnp.concatenate([q_c, kv_c, k_r_p, rest], -1)


def _layout_w_uq(w):
    n = w.shape[0]
    w = w.astype(BF16).reshape(n, MLA_Q_RANK, MLA_HEADS, MLA_NOPE + MLA_ROPE)
    x1, x2 = w[..., MLA_NOPE:MLA_NOPE + ROPE_HALF], w[..., MLA_NOPE + ROPE_HALF:]
    swapped = jnp.concatenate([jnp.zeros_like(w[..., :MLA_NOPE]), -x2, x1], -1)
    pad = ((0, 0), (0, 0), (0, 0), (0, HEAD_PAD - MLA_NOPE - MLA_ROPE))
    both = [jnp.pad(a, pad).reshape(n, MLA_Q_RANK, MLA_HEADS * HEAD_PAD) for a in (w, swapped)]
    return jnp.concatenate(both, -1)


def _layout_w_ukv(w):
    n = w.shape[0]
    w = w.astype(BF16).reshape(n, MLA_KV_RANK, MLA_HEADS, MLA_NOPE + MLA_V)
    k = jnp.pad(w[..., :MLA_NOPE], ((0, 0), (0, 0), (0, 0), (0, HEAD_PAD - MLA_NOPE)))
    v = w[..., MLA_NOPE:]
    return jnp.concatenate([k.reshape(n, MLA_KV_RANK, -1), v.reshape(n, MLA_KV_RANK, -1)], -1)


def kernel(x, meta_tokens, ln_in_g, ln_in_b, w_in, q_norm_g, w_uq, kv_norm_g, w_ukv, attn_sinks,
           grp_norm_a, grp_norm_b, w_out, ln1_g, ln1_b, w_gate_up, w_down, ln2_g, ln2_b):
    bsz, s, d = x.shape
    assert d == D_MODEL and s % ATT_TILE == 0
    lp = s + BLOCK
    tm = _row_tile(lp)
    vec = lambda a: a.reshape(1, -1)

    meta_pad = jnp.pad(meta_tokens.astype(x.dtype), ((PAD, 0), (0, 0)))
    h = _ln_in(x, meta_pad, vec(ln_in_g), vec(ln_in_b), _ln_tile(lp))
    c, sa, sb = _rope_tables(lp)
    q_scale = (MLA_NOPE + MLA_ROPE) ** -0.5 * LOG2E
    tables = (c, sa, sb, c * q_scale, (sb - sa) * q_scale)
    swa_bias = _swa_bias()

    vecs = lambda a: a[:, None, :]
    w_in_p, w_uq_p, w_ukv_p = _layout_w_in(w_in), _layout_w_uq(w_uq), _layout_w_ukv(w_ukv)
    w_out_b, w_gate_up_b, w_down_b = w_out.astype(BF16), w_gate_up.astype(BF16), w_down.astype(BF16)

    for l in range(DEPTH):
        q, k, v, sq, skd, svd = _proj(h, l, w_in_p, vecs(q_norm_g), w_uq_p, vecs(kv_norm_g), w_ukv_p,
                                      tables, tm)
        o_a = _mla(q, k, v)
        o_b = _swa(attn_sinks[l], swa_bias, sq, skd, svd, tm)
        final = l == DEPTH - 1
        h = _mix_ffn(o_a, o_b, h, l, vecs(grp_norm_a), vecs(grp_norm_b), w_out_b, vecs(ln1_g), vecs(ln1_b),
                     w_gate_up_b, w_down_b, vecs(ln2_g), vecs(ln2_b), FINAL_ROW_TILE if final else tm, final)
    return h
```

```python
import functools

import jax
import jax.numpy as jnp
import numpy as np
from jax import lax
from jax.experimental import pallas as pl
from jax.experimental.pallas import tpu as pltpu

D_MODEL = 1024
DEPTH = 4
N_META = 16
BLOCK = 128
PAD = BLOCK - N_META
MLA_HEADS = 8
MLA_Q_RANK = 256
MLA_KV_RANK = 128
MLA_NOPE = 64
MLA_ROPE = 32
MLA_V = 64
ROPE_THETA = 10000.0
SWA_HEADS = 8
SWA_KV_HEADS = 2
SWA_GROUP = SWA_HEADS // SWA_KV_HEADS
SWA_HEAD_DIM = 64
WINDOW = 128
MIX_A = MLA_HEADS * MLA_V
MIX_B = SWA_HEADS * SWA_HEAD_DIM
D_FF = 2816
ALPHA = (2 * DEPTH) ** 0.25
LN_EPS = 1e-5
RMS_EPS = 1e-6
NEG = -1e30

LANES = 128
HEAD_PAD = LANES
ROPE_HALF = MLA_ROPE // 2
ATT_TILE = 512
FINAL_ROW_TILE = ATT_TILE
LOG2E = float(np.log2(np.e))
SWA_BIAS_VARIANTS = 3

BF16 = jnp.bfloat16
F32 = jnp.float32


LN_TILE_MAX_ROWS = 1536


def _ln_tile(lp):
    n_blocks = lp // BLOCK
    return BLOCK * max(k for k in range(1, LN_TILE_MAX_ROWS // BLOCK + 1) if n_blocks % k == 0)


def _row_tile(lp):
    for t in (512, 384, 256, 128):
        if lp % t == 0:
            return t
    raise ValueError(f"padded length {lp} is not a multiple of {BLOCK}")


def _layer_norm(z, g, b):
    mu = jnp.mean(z, -1, keepdims=True)
    zc = z - mu
    var = jnp.mean(zc * zc, -1, keepdims=True)
    return zc * lax.rsqrt(var + LN_EPS) * g + b


def _rms_norm(z, g):
    return z * lax.rsqrt(jnp.mean(z * z, -1, keepdims=True) + RMS_EPS) * g


def _dot(a, b):
    return jnp.dot(a, b, preferred_element_type=F32)


def _dot_nt(a, b):
    return lax.dot_general(a, b, (((1,), (1,)), ((), ())), preferred_element_type=F32)


def _ln_in_kernel(*refs, n_sub):
    x_refs, (meta_ref, g_ref, b_ref, o_ref) = refs[:n_sub], refs[n_sub:]
    i = pl.program_id(1)
    for t, x_ref in enumerate(x_refs):
        z = x_ref[...]
        if t == 0:
            z = jnp.where(i == 0, meta_ref[...], z)
        o_ref[t * BLOCK:(t + 1) * BLOCK, :] = _layer_norm(z, g_ref[...], b_ref[...])


def _ln_in(x, meta_pad, g, b, tm):
    bsz, s, d = x.shape
    lp = s + BLOCK
    n_sub = tm // BLOCK
    x_spec = lambda t: pl.BlockSpec((None, BLOCK, d), lambda bi, i: (bi, jnp.maximum(i * n_sub + t - 1, 0), 0))
    full = lambda a: pl.BlockSpec(a.shape, lambda bi, i: (0,) * a.ndim)
    return pl.pallas_call(
        functools.partial(_ln_in_kernel, n_sub=n_sub),
        grid=(bsz, lp // tm),
        in_specs=[x_spec(t) for t in range(n_sub)] + [full(meta_pad), full(g), full(b)],
        out_specs=pl.BlockSpec((None, tm, d), lambda bi, i: (bi, i, 0)),
        out_shape=jax.ShapeDtypeStruct((bsz, lp, d), F32),
        name="ln_in",
    )(*([x] * n_sub), meta_pad, g, b)


def _rope(z, c, sa, sb):
    return z * c + pltpu.roll(z, LANES - ROPE_HALF, 1) * sa + pltpu.roll(z, ROPE_HALF, 1) * sb


def _proj_kernel(h_ref, w_in_ref, gq_ref, w_uq_ref, gkv_ref, w_ukv_ref, c_ref, sa_ref, sb_ref,
                 qcos_ref, qsin_ref, mla_ref, swa_ref):
    n_qk = MLA_HEADS * HEAD_PAD
    q_ref, k_ref, v_ref = mla_ref.at[:, :n_qk], mla_ref.at[:, n_qk:2 * n_qk], mla_ref.at[:, 2 * n_qk:]
    sq_ref = swa_ref.at[:, :MIX_B]
    sk_ref = swa_ref.at[:, MIX_B:MIX_B + 2 * LANES]
    sv_ref = swa_ref.at[:, MIX_B + 2 * LANES:]
    x = h_ref[...].astype(BF16)
    proj = _dot(x, w_in_ref[...])
    c, sa, sb = c_ref[...], sa_ref[...], sb_ref[...]
    o = 0
    q_c = proj[:, o:o + MLA_Q_RANK]; o += MLA_Q_RANK
    kv_c = proj[:, o:o + MLA_KV_RANK]; o += MLA_KV_RANK
    k_r = proj[:, o:o + LANES]; o += LANES
    s_q = proj[:, o:o + MIX_B]; o += MIX_B
    s_k = proj[:, o:o + LANES]; o += LANES
    s_v = proj[:, o:o + LANES]

    q = _dot(_rms_norm(q_c, gq_ref[...]).astype(BF16), w_uq_ref[...])
    qcos, qsin = qcos_ref[...], qsin_ref[...]
    n_q = MLA_HEADS * HEAD_PAD
    for h in range(MLA_HEADS):
        sl = slice(h * HEAD_PAD, (h + 1) * HEAD_PAD)
        swapped = slice(n_q + h * HEAD_PAD, n_q + (h + 1) * HEAD_PAD)
        q_ref[:, sl] = (q[:, sl] * qcos + q[:, swapped] * qsin).astype(BF16)

    kv = _dot(_rms_norm(kv_c, gkv_ref[...]).astype(BF16), w_ukv_ref[...])
    k_rope = _rope(k_r, c, sa, sb)
    for h in range(MLA_HEADS):
        sl = slice(h * HEAD_PAD, (h + 1) * HEAD_PAD)
        k_ref[:, sl] = (kv[:, sl] + k_rope).astype(BF16)
    v_ref[...] = kv[:, MLA_HEADS * HEAD_PAD:].astype(BF16)

    sq_ref[...] = (s_q * (SWA_HEAD_DIM ** -0.5 * LOG2E)).astype(BF16)
    low = lax.broadcasted_iota(jnp.int32, s_k.shape, 1) < SWA_HEAD_DIM
    for src, dst in ((s_k, sk_ref), (s_v, sv_ref)):
        swapped = pltpu.roll(src, SWA_HEAD_DIM, 1)
        dst[:, :LANES] = jnp.where(low, src, swapped).astype(BF16)
        dst[:, LANES:] = jnp.where(low, swapped, src).astype(BF16)


def _layer_spec(a, layer, col_block=0, n_col_blocks=1):
    _, rows, cols = a.shape
    return pl.BlockSpec((None, rows, cols // n_col_blocks), lambda bi, i: (layer, 0, col_block),
                        pipeline_mode=pl.Buffered(1))


def _proj(h, layer, w_in_p, gq, w_uq_p, gkv, w_ukv_p, tables, tm):
    bsz, lp, d = h.shape
    nt = lp // tm
    row = lambda w: pl.BlockSpec((None, tm, w), lambda i, bi: (bi, i, 0))
    tab = pl.BlockSpec((tm, LANES), lambda i, bi: (i, 0))
    params = (w_in_p, gq, w_uq_p, gkv, w_ukv_p)
    widths = (2 * MLA_HEADS * HEAD_PAD + MIX_A, MIX_B + 4 * LANES)
    return pl.pallas_call(
        _proj_kernel,
        grid=(nt, bsz),
        in_specs=[row(d)] + [_layer_spec(p, layer) for p in params] + [tab] * len(tables),
        out_specs=[row(w) for w in widths],
        out_shape=[jax.ShapeDtypeStruct((bsz, lp, w), BF16) for w in widths],
        name="proj",
    )(h, *params, *tables)


def _mla_schedule(n_tiles):
    rows = []
    for j in range(n_tiles):
        for c in range(j + 1):
            rows.append((BLOCK + j * ATT_TILE, BLOCK + c * ATT_TILE, j, int(c == 0), int(c == j)))
    return np.asarray(rows, np.int32)


def _mla_kernel(sched_ref, q_ref, k_ref, v_ref, o_ref,
                s_ref, p_ref, alpha_ref, m_ref, acc_ref, meta_m_ref, meta_acc_ref, *, n_tiles):
    n_steps = n_tiles * (n_tiles + 1) // 2
    lane = lax.broadcasted_iota(jnp.int32, (1, 2 * LANES), 1)
    head_a = (lane % LANES) < MLA_V
    tr = lax.broadcasted_iota(jnp.int32, (ATT_TILE, ATT_TILE), 0)
    tc = lax.broadcasted_iota(jnp.int32, (ATT_TILE, ATT_TILE), 1)
    diag_mask = tc <= tr

    def sched(t, col):
        return sched_ref[t, col]

    def rows_at(off):
        return pl.ds(pl.multiple_of(off, BLOCK), ATT_TILE)

    def value_matrix(vc):
        n = vc.shape[0]
        lane_i = lax.broadcasted_iota(jnp.int32, (n, LANES), 1)
        low = lane_i < MLA_V
        zero = jnp.zeros((n, LANES), BF16)
        one_lo = jnp.clip(MLA_V - lane_i, 0, 1).astype(F32)
        top = jnp.concatenate([jnp.where(low, vc, zero), one_lo.astype(BF16)], axis=1)
        bot = jnp.concatenate([jnp.where(low, zero, vc), (1.0 - one_lo).astype(BF16)], axis=1)
        return jnp.concatenate([top, bot], axis=0)

    def heads(x):
        return x[:, :HEAD_PAD], x[:, HEAD_PAD:]

    def normalized(acc):
        return (acc[:, :LANES] / acc[:, LANES:]).astype(BF16)

    def meta_attention(q, mask):
        ms, ps = [], []
        for qh, kh in zip(heads(q), heads(k_ref[:BLOCK, :])):
            s = jnp.where(mask, _dot_nt(qh, kh), NEG)
            m = jnp.max(s, -1, keepdims=True)
            ms.append(m)
            ps.append(jnp.exp2(s - m).astype(BF16))
        return ms, _dot(jnp.concatenate(ps, axis=1), value_matrix(v_ref[:BLOCK, :]))

    def score_stage(t, slot):
        q, kc = q_ref[rows_at(sched(t, 0)), :], k_ref[rows_at(sched(t, 1)), :]
        for h, (qh, kh) in enumerate(zip(heads(q), heads(kc))):
            s_ref[slot, h] = _dot_nt(qh, kh)

    def mask_stage(t, slot):
        @pl.when(sched(t, 4) == 1)
        def _():
            for h in range(2):
                s_ref[slot, h] = jnp.where(diag_mask, s_ref[slot, h], NEG)

    def softmax_stage(t, slot):
        first = sched(t, 3) == 1
        alphas = []
        for h in range(2):
            s = s_ref[slot, h]
            m_prev = jnp.where(first, meta_m_ref[sched(t, 2), h], m_ref[h])
            m_new = jnp.maximum(m_prev, jnp.max(s, -1, keepdims=True))
            alphas.append(jnp.exp2(m_prev - m_new))
            m_ref[h] = m_new
            p_ref[slot, :, h * ATT_TILE:(h + 1) * ATT_TILE] = jnp.exp2(s - m_new).astype(BF16)
        alpha_ref[slot] = jnp.where(head_a, alphas[0], alphas[1])

    def value_stage(t, slot):
        first = sched(t, 3) == 1
        pv = _dot(p_ref[slot], value_matrix(v_ref[rows_at(sched(t, 1)), :]))
        base = jnp.where(first, meta_acc_ref[sched(t, 2)], acc_ref[...])
        acc_ref[...] = alpha_ref[slot] * base + pv

    def output_stage(t):
        @pl.when(sched(t, 4) == 1)
        def _():
            o_ref[rows_at(sched(t, 0)), :] = normalized(acc_ref[...])

    def step(t, slot):
        score_stage(t + 1, 1 - slot)
        value_stage(t - 1, 1 - slot)
        softmax_stage(t, slot)
        output_stage(t - 1)
        mask_stage(t + 1, 1 - slot)

    r = lax.broadcasted_iota(jnp.int32, (BLOCK, BLOCK), 0)
    cidx = lax.broadcasted_iota(jnp.int32, (BLOCK, BLOCK), 1)
    _, acc0 = meta_attention(q_ref[:BLOCK, :], (cidx <= r) & (cidx >= PAD))
    o_ref[:BLOCK, :] = normalized(acc0)

    meta_mask = lax.broadcasted_iota(jnp.int32, (ATT_TILE, BLOCK), 1) >= PAD
    for j in range(n_tiles):
        ms, acc = meta_attention(q_ref[BLOCK + j * ATT_TILE:BLOCK + (j + 1) * ATT_TILE, :], meta_mask)
        meta_acc_ref[j] = acc
        for h in range(2):
            meta_m_ref[j, h] = ms[h]

    score_stage(0, 0)
    mask_stage(0, 0)
    if n_steps == 1:
        softmax_stage(0, 0)
    else:
        score_stage(1, 1)
        softmax_stage(0, 0)
        mask_stage(1, 1)
        n_mid = n_steps - 2

        def two_steps(i, carry):
            t = 1 + 2 * i
            step(t, 1)
            step(t + 1, 0)
            return carry

        lax.fori_loop(0, n_mid // 2, two_steps, 0)
        if n_mid % 2:
            step(n_steps - 2, (n_steps - 2) % 2)
        last = n_steps - 1
        value_stage(last - 1, (last - 1) % 2)
        softmax_stage(last, last % 2)
        output_stage(last - 1)
    value_stage(n_steps - 1, (n_steps - 1) % 2)
    output_stage(n_steps - 1)


def _mla(qkv):
    bsz, lp, _ = qkv.shape
    n_pairs = MLA_HEADS // 2
    n_tiles = (lp - BLOCK) // ATT_TILE
    sched = jnp.asarray(_mla_schedule(n_tiles))
    q_spec = pl.BlockSpec((None, lp, 2 * HEAD_PAD), lambda bi, p: (bi, 0, p))
    k_spec = pl.BlockSpec((None, lp, 2 * HEAD_PAD), lambda bi, p: (bi, 0, n_pairs + p))
    v_in_spec = pl.BlockSpec((None, lp, 2 * MLA_V), lambda bi, p: (bi, 0, 2 * n_pairs * HEAD_PAD // MLA_V + p))
    v_spec = pl.BlockSpec((None, lp, 2 * MLA_V), lambda bi, p: (bi, 0, p))
    return pl.pallas_call(
        functools.partial(_mla_kernel, n_tiles=n_tiles),
        grid=(bsz, n_pairs),
        in_specs=[pl.BlockSpec(memory_space=pltpu.SMEM), q_spec, k_spec, v_in_spec],
        out_specs=v_spec,
        out_shape=jax.ShapeDtypeStruct((bsz, lp, MIX_A), BF16),
        scratch_shapes=[
            pltpu.VMEM((2, 2, ATT_TILE, ATT_TILE), F32),
            pltpu.VMEM((2, ATT_TILE, 2 * ATT_TILE), BF16),
            pltpu.VMEM((2, ATT_TILE, 2 * LANES), F32),
            pltpu.VMEM((2, ATT_TILE, 1), F32),
            pltpu.VMEM((ATT_TILE, 2 * LANES), F32),
            pltpu.VMEM((n_tiles, 2, ATT_TILE, 1), F32),
            pltpu.VMEM((n_tiles, ATT_TILE, 2 * LANES), F32),
        ],
        name="mla",
    )(sched, qkv, qkv, qkv)


def _swa_kernel(sinks_ref, bias_ref, q_ref, kp_ref, kc_ref, vp_ref, vc_ref, o_ref, *, n_sub):
    first_block = pl.program_id(1) * n_sub
    low = lax.broadcasted_iota(jnp.int32, (BLOCK, LANES), 1) < SWA_HEAD_DIM
    zero = jnp.zeros((BLOCK, LANES), BF16)

    for t in range(n_sub):
        rows = slice(t * BLOCK, (t + 1) * BLOCK)
        variant = jnp.minimum(first_block + t, SWA_BIAS_VARIANTS - 1)
        for kvh in range(SWA_KV_HEADS):
            sl = slice(kvh * LANES, (kvh + 1) * LANES)
            if t == 0:
                k_band = jnp.concatenate([kp_ref[:, sl], kc_ref[rows, sl]], axis=0)
                v_band = jnp.concatenate([vp_ref[:, sl], vc_ref[rows, sl]], axis=0)
            else:
                band = slice((t - 1) * BLOCK, (t + 1) * BLOCK)
                k_band, v_band = kc_ref[band, sl], vc_ref[band, sl]
            q_rows = []
            for pair in range(SWA_GROUP // 2):
                g = kvh * (SWA_GROUP // 2) + pair
                qp = q_ref[rows, g * LANES:(g + 1) * LANES]
                q_rows += [jnp.where(low, qp, zero), jnp.where(low, zero, qp)]
            s_all = _dot_nt(jnp.concatenate(q_rows, axis=0), k_band)
            probs = []
            for gi in range(SWA_GROUP):
                head = kvh * SWA_GROUP + gi
                sink = sinks_ref[head] * LOG2E
                s = s_all[gi * BLOCK:(gi + 1) * BLOCK] + bias_ref[variant, head]
                m = jnp.maximum(jnp.max(s, -1, keepdims=True), sink)
                e = jnp.exp2(s - m)
                p = e / (jnp.sum(e, -1, keepdims=True) + jnp.exp2(sink - m))
                probs.append(p.astype(BF16))
            o_all = _dot(jnp.concatenate(probs, axis=0), v_band)
            for pair in range(SWA_GROUP // 2):
                g = kvh * (SWA_GROUP // 2) + pair
                o_lo = o_all[(2 * pair) * BLOCK:(2 * pair + 1) * BLOCK]
                o_hi = o_all[(2 * pair + 1) * BLOCK:(2 * pair + 2) * BLOCK]
                o_ref[rows, g * LANES:(g + 1) * LANES] = jnp.where(low, o_lo, o_hi).astype(BF16)


def _swa_bias():
    row = jnp.arange(BLOCK)[:, None]
    col = jnp.arange(2 * BLOCK)[None, :]
    dist = BLOCK + row - col
    in_band = (dist >= 0) & (dist < WINDOW)
    slopes = jnp.exp2(-8.0 * jnp.arange(1, SWA_HEADS + 1, dtype=F32) / SWA_HEADS)
    alibi = -slopes[:, None, None] * dist.astype(F32)[None] * LOG2E
    variants = []
    for block in range(SWA_BIAS_VARIANTS):
        valid = in_band & ((block - 1) * BLOCK + col >= PAD)
        variants.append(jnp.where(valid[None], alibi, NEG))
    return jnp.stack(variants)


def _swa(sinks, bias, sqkv, tm):
    bsz, lp, _ = sqkv.shape
    n_sub = tm // BLOCK
    sq = skd = svd = sqkv
    k_blk, v_blk = MIX_B // (2 * LANES), MIX_B // (2 * LANES) + 1
    cur = lambda w, c=0: pl.BlockSpec((None, tm, w), lambda bi, i: (bi, i, c))
    prev = lambda w, c: pl.BlockSpec((None, BLOCK, w), lambda bi, i: (bi, jnp.maximum(i * n_sub - 1, 0), c))
    return pl.pallas_call(
        functools.partial(_swa_kernel, n_sub=n_sub),
        grid=(bsz, lp // tm),
        in_specs=[pl.BlockSpec(memory_space=pltpu.SMEM),
                  pl.BlockSpec(bias.shape, lambda bi, i: (0, 0, 0, 0), pipeline_mode=pl.Buffered(1)),
                  cur(MIX_B), prev(2 * LANES, k_blk), cur(2 * LANES, k_blk),
                  prev(2 * LANES, v_blk), cur(2 * LANES, v_blk)],
        out_specs=cur(MIX_B),
        out_shape=jax.ShapeDtypeStruct((bsz, lp, MIX_B), BF16),
        name="swa",
    )(sinks, bias, sq, skd, skd, svd, svd)


def _mix_ffn_kernel(*refs, n_in):
    rows = lambda k: refs[k * n_in:(k + 1) * n_in]
    (ga_ref, gb_ref, wo_ref, g1_ref, b1_ref, wg_ref, wu_ref, wd_ref, g2_ref, b2_ref, o_ref) = refs[3 * n_in:]
    tile = lambda rs: jnp.concatenate([r[...] for r in rs], axis=0) if n_in > 1 else rs[0][...]
    a = _rms_norm(tile(rows(0)).astype(F32), ga_ref[...]).astype(BF16)
    b = _rms_norm(tile(rows(1)).astype(F32), gb_ref[...]).astype(BF16)
    y = _dot(jnp.concatenate([a, b], axis=-1), wo_ref[...])
    x = _layer_norm(ALPHA * tile(rows(2)) + y, g1_ref[...], b1_ref[...])
    xb = x.astype(BF16)
    gate = _dot(xb, wg_ref[...])
    up = _dot(xb, wu_ref[...])
    act = (gate * (1.0 / (1.0 + jnp.exp(-gate))) * up).astype(BF16)
    f = _dot(act, wd_ref[...])
    o_ref[...] = _layer_norm(ALPHA * x + f, g2_ref[...], b2_ref[...])


def _mix_ffn_vmem_bytes(tm, d):
    weights = 2 * ((MIX_A + MIX_B) * d + d * 2 * D_FF + D_FF * d)
    row_tiles = 2 * (tm * (MIX_A + MIX_B) * 2 + 2 * tm * d * 4)
    temporaries = tm * (2 * D_FF * 4 + D_FF * 2 + 4 * d * 4)
    return weights + row_tiles + temporaries


def _mix_ffn(o_a, o_b, h, layer, ga, gb, w_out, g1, b1, w_gate_up, w_down, g2, b2, tm, final):
    bsz, lp, d = h.shape
    spec = lambda a: _layer_spec(a, layer)
    param_specs = [spec(ga), spec(gb), spec(w_out), spec(g1), spec(b1),
                   _layer_spec(w_gate_up, layer, 0, 2), _layer_spec(w_gate_up, layer, 1, 2),
                   spec(w_down), spec(g2), spec(b2)]
    params = (ga, gb, w_out, g1, b1, w_gate_up, w_gate_up, w_down, g2, b2)
    if final:
        rows, n_in = lp - BLOCK, tm // BLOCK
        row_specs = lambda w: [pl.BlockSpec((None, BLOCK, w), lambda bi, i, t=t: (bi, 1 + i * n_in + t, 0))
                               for t in range(n_in)]
    else:
        rows, n_in = lp, 1
        row_specs = lambda w: [pl.BlockSpec((None, tm, w), lambda bi, i: (bi, i, 0))]
    return pl.pallas_call(
        functools.partial(_mix_ffn_kernel, n_in=n_in),
        grid=(bsz, rows // tm),
        in_specs=row_specs(MIX_A) + row_specs(MIX_B) + row_specs(d) + param_specs,
        out_specs=pl.BlockSpec((None, tm, d), lambda bi, i: (bi, i, 0)),
        out_shape=jax.ShapeDtypeStruct((bsz, rows, d), F32),
        compiler_params=pltpu.CompilerParams(vmem_limit_bytes=_mix_ffn_vmem_bytes(tm, d)),
        name="mix_ffn",
    )(*([o_a] * n_in), *([o_b] * n_in), *([h] * n_in), *params)


def _rope_tables(lp):
    pos = jnp.maximum(jnp.arange(lp, dtype=jnp.int32) - PAD, 0).astype(F32)
    inv = ROPE_THETA ** (-jnp.arange(0, MLA_ROPE, 2, dtype=F32) / MLA_ROPE)
    ang = pos[:, None] * inv[None, :]
    cos, sin = jnp.cos(ang), jnp.sin(ang)
    z16 = jnp.zeros((lp, ROPE_HALF), F32)
    z32 = jnp.zeros((lp, LANES - MLA_NOPE - MLA_ROPE), F32)
    one = jnp.ones((lp, MLA_NOPE), F32)
    z64 = jnp.zeros((lp, MLA_NOPE), F32)
    c = jnp.concatenate([one, cos, cos, z32], -1)
    sa = jnp.concatenate([z64, -sin, z16, z32], -1)
    sb = jnp.concatenate([z64, z16, sin, z32], -1)
    return c, sa, sb


def _layout_w_in(w):
    w = w.astype(BF16)
    o = 0
    q_c = w[..., o:o + MLA_Q_RANK]; o += MLA_Q_RANK
    kv_c = w[..., o:o + MLA_KV_RANK]; o += MLA_KV_RANK
    k_r = w[..., o:o + MLA_ROPE]; o += MLA_ROPE
    rest = w[..., o:]
    k_r_p = jnp.concatenate([jnp.zeros(w.shape[:-1] + (MLA_NOPE,), BF16), k_r,
                             jnp.zeros(w.shape[:-1] + (LANES - MLA_NOPE - MLA_ROPE,), BF16)], -1)
    return jnp.concatenate([q_c, kv_c, k_r_p, rest], -1)


def _layout_w_uq(w):
    n = w.shape[0]
    w = w.astype(BF16).reshape(n, MLA_Q_RANK, MLA_HEADS, MLA_NOPE + MLA_ROPE)
    x1, x2 = w[..., MLA_NOPE:MLA_NOPE + ROPE_HALF], w[..., MLA_NOPE + ROPE_HALF:]
    swapped = jnp.concatenate([jnp.zeros_like(w[..., :MLA_NOPE]), -x2, x1], -1)
    pad = ((0, 0), (0, 0), (0, 0), (0, HEAD_PAD - MLA_NOPE - MLA_ROPE))
    both = [jnp.pad(a, pad).reshape(n, MLA_Q_RANK, MLA_HEADS * HEAD_PAD) for a in (w, swapped)]
    return jnp.concatenate(both, -1)


def _layout_w_ukv(w):
    n = w.shape[0]
    w = w.astype(BF16).reshape(n, MLA_KV_RANK, MLA_HEADS, MLA_NOPE + MLA_V)
    k = jnp.pad(w[..., :MLA_NOPE], ((0, 0), (0, 0), (0, 0), (0, HEAD_PAD - MLA_NOPE)))
    v = w[..., MLA_NOPE:]
    return jnp.concatenate([k.reshape(n, MLA_KV_RANK, -1), v.reshape(n, MLA_KV_RANK, -1)], -1)


def kernel(x, meta_tokens, ln_in_g, ln_in_b, w_in, q_norm_g, w_uq, kv_norm_g, w_ukv, attn_sinks,
           grp_norm_a, grp_norm_b, w_out, ln1_g, ln1_b, w_gate_up, w_down, ln2_g, ln2_b):
    bsz, s, d = x.shape
    assert d == D_MODEL and s % ATT_TILE == 0
    lp = s + BLOCK
    tm = _row_tile(lp)
    vec = lambda a: a.reshape(1, -1)

    meta_pad = jnp.pad(meta_tokens.astype(x.dtype), ((PAD, 0), (0, 0)))
    h = _ln_in(x, meta_pad, vec(ln_in_g), vec(ln_in_b), _ln_tile(lp))
    c, sa, sb = _rope_tables(lp)
    q_scale = (MLA_NOPE + MLA_ROPE) ** -0.5 * LOG2E
    tables = (c, sa, sb, c * q_scale, (sb - sa) * q_scale)
    swa_bias = _swa_bias()

    vecs = lambda a: a[:, None, :]
    w_in_p, w_uq_p, w_ukv_p = _layout_w_in(w_in), _layout_w_uq(w_uq), _layout_w_ukv(w_ukv)
    w_out_b, w_gate_up_b, w_down_b = w_out.astype(BF16), w_gate_up.astype(BF16), w_down.astype(BF16)

    for l in range(DEPTH):
        qkv, sqkv = _proj(h, l, w_in_p, vecs(q_norm_g), w_uq_p, vecs(kv_norm_g), w_ukv_p, tables, tm)
        o_a = _mla(qkv)
        o_b = _swa(attn_sinks[l], swa_bias, sqkv, tm)
        final = l == DEPTH - 1
        h = _mix_ffn(o_a, o_b, h, l, vecs(grp_norm_a), vecs(grp_norm_b), w_out_b, vecs(ln1_g), vecs(ln1_b),
                     w_gate_up_b, w_down_b, vecs(ln2_g), vecs(ln2_b), FINAL_ROW_TILE if final else tm, final)
    return h
```
